```python
import math
import jax, jax.numpy as jnp
from jax import lax
import numpy as np

D_MODEL = 1024
BATCH = 2
SEQ = 16384
DEPTH = 1
DEC_BATCH = 2
DEC_SEQ = 8192
PAST_LEN = 128

GRID_W = 64
SSM_EXPAND = 2
D_INNER = SSM_EXPAND * D_MODEL
SSM_HEAD_DIM = 64
SSM_HEADS = D_INNER // SSM_HEAD_DIM
SSM_GROUPS = 4
SSM_HEADS_PER_GROUP = SSM_HEADS // SSM_GROUPS
D_STATE = 128
D_CONV = 5
CHUNK = 128
CONV_DIM = D_INNER + 2 * SSM_GROUPS * D_STATE
ATTN_HEAD_DIM = 128
N_Q_HEADS = 8
N_KV_HEADS = 2
Q_PER_KV = N_Q_HEADS // N_KV_HEADS
ATTN_WIDTH = N_Q_HEADS * ATTN_HEAD_DIM
KV_WIDTH = N_KV_HEADS * ATTN_HEAD_DIM
Q_BLOCK = 128
ROPE_THETA = 10000.0
D_FF = 2816
N_BRANCH = 2
EPS = 1e-6
IN_SIZES = (D_INNER, CONV_DIM, SSM_HEADS, SSM_HEADS, ATTN_WIDTH, KV_WIDTH, KV_WIDTH, N_BRANCH * D_MODEL)
IN_PROJ_DIM = sum(IN_SIZES)
IN_SPLITS = tuple(int(v) for v in np.cumsum(IN_SIZES)[:-1])

kernel_name = 'hybrid_ssd_axial_gqa_macaron_encoder'


def rmsnorm(x, g):
    xf = x.astype(jnp.float32)
    y = xf * lax.rsqrt(jnp.mean(xf * xf, axis=-1, keepdims=True) + EPS)
    return (y * g.astype(jnp.float32)).astype(x.dtype)


def swiglu(x, w_gu, w_down):
    g, u = jnp.split(x @ w_gu, 2, axis=-1)
    return (jax.nn.silu(g) * u) @ w_down


def depthwise_conv_centred(x, w, b):
    c = x.shape[-1]
    pad = D_CONV // 2
    out = lax.conv_general_dilated(x, w[:, None, :].astype(x.dtype), window_strides=(1,),
                                   padding=[(pad, pad)], dimension_numbers=('NWC', 'WIO', 'NWC'),
                                   feature_group_count=c)
    return out + b.astype(x.dtype)


def ssd_scan(x, dt, A, B, C):
    f32 = jnp.float32
    b, L, H, P = x.shape
    nc = L // CHUNK
    G, E = SSM_GROUPS, SSM_HEADS_PER_GROUP
    xc = (x.astype(f32) * dt[..., None]).reshape(b, nc, CHUNK, G, E, P)
    Bc = B.astype(f32).reshape(b, nc, CHUNK, G, D_STATE)
    Cc = C.astype(f32).reshape(b, nc, CHUNK, G, D_STATE)
    a = (dt * A).reshape(b, nc, CHUNK, G, E).transpose(0, 1, 3, 4, 2)
    acum = jnp.cumsum(a, axis=-1)
    tril = jnp.tril(jnp.ones((CHUNK, CHUNK), dtype=bool))
    seg = acum[..., :, None] - acum[..., None, :]
    Lmat = jnp.exp(jnp.where(tril, seg, -jnp.inf))
    CB = jnp.einsum('bclgn,bcsgn->bcgls', Cc, Bc)
    y_diag = jnp.einsum('bcgls,bcgels,bcsgep->bclgep', CB, Lmat, xc)
    decay_states = jnp.exp(acum[..., -1:] - acum)
    states = jnp.einsum('bclgn,bcgel,bclgep->bcgepn', Bc, decay_states, xc)
    chunk_decay = jnp.exp(acum[..., -1])

    def step(h, inp):
        s_c, d_c = inp
        return h * d_c[..., None, None] + s_c, h

    h0 = jnp.zeros((b, G, E, P, D_STATE), f32)
    _, prev = lax.scan(step, h0, (states.transpose(1, 0, 2, 3, 4, 5), chunk_decay.transpose(1, 0, 2, 3)))
    prev = prev.transpose(1, 0, 2, 3, 4, 5)
    y_off = jnp.einsum('bclgn,bcgepn,bcgel->bclgep', Cc, prev, jnp.exp(acum))
    return (y_diag + y_off).reshape(b, L, H, P)


def ssd_branch(z, xBC, dt_f, dt_b, conv_w, conv_b, dt_bias_f, dt_bias_b, A_log_f, A_log_b, D_skip, ssm_norm):
    f32 = jnp.float32
    b, L, _ = xBC.shape
    xBC = jax.nn.silu(depthwise_conv_centred(xBC, conv_w, conv_b))
    xs, Bm, Cm = jnp.split(xBC, [D_INNER, D_INNER + SSM_GROUPS * D_STATE], axis=-1)
    xs = xs.reshape(b, L, SSM_HEADS, SSM_HEAD_DIM)
    Bm = Bm.reshape(b, L, SSM_GROUPS, D_STATE)
    Cm = Cm.reshape(b, L, SSM_GROUPS, D_STATE)
    dtf = jax.nn.softplus(dt_f.astype(f32) + dt_bias_f.astype(f32))
    dtb = jax.nn.softplus(dt_b.astype(f32) + dt_bias_b.astype(f32))
    Af = -jnp.exp(A_log_f.astype(f32))
    Ab = -jnp.exp(A_log_b.astype(f32))
    flip = lambda t: jnp.flip(t, axis=1)
    y_fwd = ssd_scan(xs, dtf, Af, Bm, Cm)
    y_bwd = flip(ssd_scan(flip(xs), flip(dtb), Ab, flip(Bm), flip(Cm)))
    y = y_fwd + y_bwd + D_skip.astype(f32)[:, None] * xs.astype(f32)
    y = y.reshape(b, L, D_INNER) * jax.nn.silu(z.astype(f32))
    y = y.reshape(b, L, SSM_GROUPS, D_INNER // SSM_GROUPS)
    y = y * lax.rsqrt(jnp.mean(y * y, axis=-1, keepdims=True) + EPS)
    return (y.reshape(b, L, D_INNER) * ssm_norm.astype(f32)).astype(z.dtype)


def axial_rope_tables(L):
    rows = L // GRID_W
    row = jnp.repeat(jnp.arange(rows, dtype=jnp.float32), GRID_W)
    col = jnp.tile(jnp.arange(GRID_W, dtype=jnp.float32), rows)
    axis_dim = ATTN_HEAD_DIM // 2
    inv_freq = ROPE_THETA ** (-jnp.arange(0, axis_dim, 2, dtype=jnp.float32) / axis_dim)
    ang = jnp.concatenate([row[:, None] * inv_freq, col[:, None] * inv_freq], axis=-1)
    return jnp.cos(ang), jnp.sin(ang)


def apply_rope(x, cos, sin):
    xf = x.astype(jnp.float32).reshape(*x.shape[:-1], ATTN_HEAD_DIM // 2, 2)
    x0, x1 = xf[..., 0], xf[..., 1]
    c, s = cos[:, None, :], sin[:, None, :]
    out = jnp.stack([x0 * c - x1 * s, x0 * s + x1 * c], axis=-1)
    return out.reshape(x.shape).astype(x.dtype)


def block_attention(q, k, v):
    b, L, _, dh = q.shape
    nblk = L // Q_BLOCK
    qb = q.reshape(b, nblk, Q_BLOCK, N_KV_HEADS, Q_PER_KV, dh).transpose(1, 0, 2, 3, 4, 5)
    scale = dh ** -0.5

    def one_block(qblk):
        s = jnp.einsum('bqhgd,bshd->bhgqs', qblk, k, preferred_element_type=jnp.float32) * scale
        p = jax.nn.softmax(s, axis=-1)
        return jnp.einsum('bhgqs,bshd->bqhgd', p.astype(v.dtype), v)

    o = lax.map(one_block, qb)
    return o.transpose(1, 0, 2, 3, 4, 5).reshape(b, L, ATTN_WIDTH)


def encoder_layer(x, norm_ffn1, w_ffn1_gu, w_ffn1_down, norm_mix, w_in, conv_w, conv_b,
                  dt_bias_f, dt_bias_b, A_log_f, A_log_b, D_skip, ssm_norm, q_norm, k_norm,
                  w_ssm_branch, w_attn_branch, b_gate, w_out, norm_ffn2, w_ffn2_gu, w_ffn2_down):
    b, L, _ = x.shape
    h = x + 0.5 * swiglu(rmsnorm(x, norm_ffn1), w_ffn1_gu, w_ffn1_down)
    u = rmsnorm(h, norm_mix)
    z, xBC, dt_f, dt_b, q, k, v, gate_logits = jnp.split(u @ w_in, IN_SPLITS, axis=-1)
    s = ssd_branch(z, xBC, dt_f, dt_b, conv_w, conv_b, dt_bias_f, dt_bias_b, A_log_f, A_log_b, D_skip, ssm_norm)
    q = rmsnorm(q.reshape(b, L, N_Q_HEADS, ATTN_HEAD_DIM), q_norm)
    k = rmsnorm(k.reshape(b, L, N_KV_HEADS, ATTN_HEAD_DIM), k_norm)
    v = v.reshape(b, L, N_KV_HEADS, ATTN_HEAD_DIM)
    cos, sin = axial_rope_tables(L)
    a = block_attention(apply_rope(q, cos, sin), apply_rope(k, cos, sin), v)
    g = jax.nn.sigmoid((gate_logits + b_gate).astype(jnp.float32)).astype(x.dtype)
    g_s, g_a = jnp.split(g, N_BRANCH, axis=-1)
    m = g_s * (s @ w_ssm_branch) + g_a * (a @ w_attn_branch)
    h = h + m @ w_out
    return h + 0.5 * swiglu(rmsnorm(h, norm_ffn2), w_ffn2_gu, w_ffn2_down)


def run_trunk(x, norm_ffn1, w_ffn1_gu, w_ffn1_down, norm_mix, w_in, conv_w, conv_b,
              dt_bias_f, dt_bias_b, A_log_f, A_log_b, D_skip, ssm_norm, q_norm, k_norm,
              w_ssm_branch, w_attn_branch, b_gate, w_out, norm_ffn2, w_ffn2_gu, w_ffn2_down, norm_final):
    h = x
    for i in range(DEPTH):
        h = encoder_layer(h, norm_ffn1[i], w_ffn1_gu[i], w_ffn1_down[i], norm_mix[i], w_in[i],
                          conv_w[i], conv_b[i], dt_bias_f[i], dt_bias_b[i], A_log_f[i], A_log_b[i],
                          D_skip[i], ssm_norm[i], q_norm[i], k_norm[i], w_ssm_branch[i],
                          w_attn_branch[i], b_gate[i], w_out[i], norm_ffn2[i], w_ffn2_gu[i], w_ffn2_down[i])
    return rmsnorm(h, norm_final)


def setup_inputs(seed: int = 0) -> dict:
    key = jax.random.key(seed)
    ks = jax.random.split(key, 32)
    f32 = jnp.float32
    Ld = DEPTH

    def nrm(k, shape, fan_in):
        return jax.random.normal(k, shape, f32) * (fan_in ** -0.5)

    def gain(k, shape):
        return 1.0 + 0.02 * jax.random.normal(k, shape, f32)

    def dt_bias_init(k):
        dt0 = jnp.exp(jax.random.uniform(k, (Ld, SSM_HEADS), f32, minval=math.log(1e-3), maxval=math.log(1e-1)))
        return dt0 + jnp.log(-jnp.expm1(-dt0))

    def a_log_init(k):
        return jnp.log(jax.random.uniform(k, (Ld, SSM_HEADS), f32, minval=1.0, maxval=16.0))

    return {
        'x_prompt': jax.random.normal(ks[0], (BATCH, SEQ, D_MODEL), f32),
        'x_sample': jax.random.normal(ks[1], (DEC_BATCH, DEC_SEQ, D_MODEL), f32),
        'norm_ffn1': gain(ks[2], (Ld, D_MODEL)),
        'w_ffn1_gu': nrm(ks[3], (Ld, D_MODEL, 2 * D_FF), D_MODEL),
        'w_ffn1_down': nrm(ks[4], (Ld, D_FF, D_MODEL), D_FF),
        'norm_mix': gain(ks[5], (Ld, D_MODEL)),
        'w_in': nrm(ks[6], (Ld, D_MODEL, IN_PROJ_DIM), D_MODEL),
        'conv_w': nrm(ks[7], (Ld, D_CONV, CONV_DIM), D_CONV),
        'conv_b': 0.01 * jax.random.normal(ks[8], (Ld, CONV_DIM), f32),
        'dt_bias_f': dt_bias_init(ks[9]),
        'dt_bias_b': dt_bias_init(ks[10]),
        'A_log_f': a_log_init(ks[11]),
        'A_log_b': a_log_init(ks[12]),
        'D_skip': gain(ks[13], (Ld, SSM_HEADS)),
        'ssm_norm': gain(ks[14], (Ld, D_INNER)),
        'q_norm': gain(ks[15], (Ld, ATTN_HEAD_DIM)),
        'k_norm': gain(ks[16], (Ld, ATTN_HEAD_DIM)),
        'w_ssm_branch': nrm(ks[17], (Ld, D_INNER, D_MODEL), D_INNER),
        'w_attn_branch': nrm(ks[18], (Ld, ATTN_WIDTH, D_MODEL), ATTN_WIDTH),
        'b_gate': 0.01 * jax.random.normal(ks[19], (Ld, N_BRANCH * D_MODEL), f32),
        'w_out': nrm(ks[20], (Ld, D_MODEL, D_MODEL), D_MODEL),
        'norm_ffn2': gain(ks[21], (Ld, D_MODEL)),
        'w_ffn2_gu': nrm(ks[22], (Ld, D_MODEL, 2 * D_FF), D_MODEL),
        'w_ffn2_down': nrm(ks[23], (Ld, D_FF, D_MODEL), D_FF),
        'norm_final': gain(ks[24], (D_MODEL,)),
    }


def reference(x_prompt, x_sample, norm_ffn1, w_ffn1_gu, w_ffn1_down, norm_mix, w_in, conv_w, conv_b,
              dt_bias_f, dt_bias_b, A_log_f, A_log_b, D_skip, ssm_norm, q_norm, k_norm,
              w_ssm_branch, w_attn_branch, b_gate, w_out, norm_ffn2, w_ffn2_gu, w_ffn2_down, norm_final):
    y_prompt = run_trunk(x_prompt, norm_ffn1, w_ffn1_gu, w_ffn1_down, norm_mix, w_in, conv_w, conv_b,
                         dt_bias_f, dt_bias_b, A_log_f, A_log_b, D_skip, ssm_norm, q_norm, k_norm,
                         w_ssm_branch, w_attn_branch, b_gate, w_out, norm_ffn2, w_ffn2_gu, w_ffn2_down, norm_final)
    y_sample = run_trunk(x_sample, norm_ffn1, w_ffn1_gu, w_ffn1_down, norm_mix, w_in, conv_w, conv_b,
                         dt_bias_f, dt_bias_b, A_log_f, A_log_b, D_skip, ssm_norm, q_norm, k_norm,
                         w_ssm_branch, w_attn_branch, b_gate, w_out, norm_ffn2, w_ffn2_gu, w_ffn2_down, norm_final)
    return (y_prompt, y_sample)
```

```python
import functools

import jax
import jax.numpy as jnp
import numpy as np
from jax import lax
from jax.experimental import pallas as pl
from jax.experimental.pallas import tpu as pltpu

F32 = jnp.float32
BF16 = jnp.bfloat16

D_MODEL = 1024
GRID_W = 64
D_INNER = 2048
SSM_HEAD_DIM = 64
SSM_HEADS = 32
SSM_GROUPS = 4
GROUP_WIDTH = D_INNER // SSM_GROUPS
D_STATE = 128
D_CONV = 5
CHUNK = 128
BC_WIDTH = SSM_GROUPS * D_STATE
CONV_DIM = D_INNER + 2 * BC_WIDTH
HEAD_DIM = 128
N_Q_HEADS = 8
N_KV_HEADS = 2
Q_PER_KV = N_Q_HEADS // N_KV_HEADS
ATTN_WIDTH = N_Q_HEADS * HEAD_DIM
KV_WIDTH = N_KV_HEADS * HEAD_DIM
ROPE_THETA = 10000.0
D_FF = 2816
EPS = 1e-6
DT_PAD = 128

LANES = 128
SUBLANES = 8
VMEM_LIMIT = 56 * 1024 * 1024

_OFF_Z = 0
_OFF_XBC = _OFF_Z + D_INNER
_OFF_Q = _OFF_XBC + CONV_DIM
_OFF_K = _OFF_Q + ATTN_WIDTH
_OFF_V = _OFF_K + KV_WIDTH
_OFF_G = _OFF_V + KV_WIDTH
_OFF_DT = _OFF_G + 2 * D_MODEL
_IN_COLS = _OFF_DT + DT_PAD


def _rms(x, g):
    return x * lax.rsqrt(jnp.mean(x * x, axis=-1, keepdims=True) + EPS) * g


def _silu(x):
    return x * jax.nn.sigmoid(x)


def _softplus(x):
    return jnp.maximum(x, 0.0) + jnp.log1p(jnp.exp(-jnp.abs(x)))


def _params(*sem):
    return pltpu.CompilerParams(dimension_semantics=sem, vmem_limit_bytes=VMEM_LIMIT)


def _resident(shape):
    nd = len(shape)
    return pl.BlockSpec(shape, lambda *_: (0,) * nd, pipeline_mode=pl.Buffered(1))


def _ffn_kernel(x_ref, gpre_ref, wgu_ref, wd_ref, gpost_ref, *out_refs, tf, emit_h):
    x = x_ref[...]
    xn = _rms(x, gpre_ref[...]).astype(BF16)
    acc = jnp.zeros_like(x)
    for c in range(D_FF // tf):
        g = jnp.dot(xn, wgu_ref[:, c * tf:(c + 1) * tf], preferred_element_type=F32)
        u = jnp.dot(xn, wgu_ref[:, D_FF + c * tf:D_FF + (c + 1) * tf], preferred_element_type=F32)
        act = (_silu(g) * u).astype(BF16)
        acc = acc + jnp.dot(act, wd_ref[c * tf:(c + 1) * tf, :], preferred_element_type=F32)
    h = x + 0.5 * acc
    if emit_h:
        out_refs[0][...] = h
    post_ref = out_refs[-1]
    post_ref[...] = _rms(h, gpost_ref[...]).astype(post_ref.dtype)


def _ffn(x, g_pre, w_gu, w_down, g_post, *, emit_h, post_dtype, tm=512, tf=256):
    b, L, _ = x.shape
    tok = pl.BlockSpec((None, tm, D_MODEL), lambda bi, i: (bi, i, 0))
    out_shape = [jax.ShapeDtypeStruct((b, L, D_MODEL), post_dtype)]
    out_specs = [tok]
    if emit_h:
        out_shape.insert(0, jax.ShapeDtypeStruct((b, L, D_MODEL), F32))
        out_specs.insert(0, tok)
    return pl.pallas_call(
        functools.partial(_ffn_kernel, tf=tf, emit_h=emit_h),
        grid=(b, L // tm),
        in_specs=[tok, _resident((1, D_MODEL)), _resident((D_MODEL, 2 * D_FF)),
                  _resident((D_FF, D_MODEL)), _resident((1, D_MODEL))],
        out_specs=out_specs,
        out_shape=out_shape,
        compiler_params=_params("parallel", "parallel"),
        name="ffn",
    )(x, g_pre, w_gu, w_down, g_post)


def _inproj_kernel(u_ref, w_ref, wdt_t_ref, cos_ref, sin_ref, qg_ref, kg_ref, bg_ref,
                   dtb_row_ref, dtb_col_ref,
                   z_ref, xbc_ref, q_ref, k_ref, v_ref, g_ref, dt_ref, dtt_ref):
    u = u_ref[...]

    def proj(off, width):
        return jnp.dot(u, w_ref[:, off:off + width], preferred_element_type=F32)

    z_ref[...] = proj(_OFF_Z, D_INNER)
    xbc_ref[...] = proj(_OFF_XBC, CONV_DIM)
    v_ref[...] = proj(_OFF_V, KV_WIDTH).astype(BF16)
    g_ref[...] = jax.nn.sigmoid(proj(_OFF_G, 2 * D_MODEL) + bg_ref[...])
    dt_ref[...] = _softplus(proj(_OFF_DT, DT_PAD) + dtb_row_ref[...])
    dtt = lax.dot_general(wdt_t_ref[...], u, (((1,), (1,)), ((), ())), preferred_element_type=F32)
    dtt_ref[...] = _softplus(dtt + dtb_col_ref[...])

    cos = cos_ref[...]
    sin = sin_ref[...]

    def norm_rope(x, gain):
        x = _rms(x, gain)
        return x * cos + pltpu.roll(x, HEAD_DIM // 2, axis=1) * sin

    scale = HEAD_DIM ** -0.5
    for h in range(N_Q_HEADS):
        qh = proj(_OFF_Q + h * HEAD_DIM, HEAD_DIM)
        q_ref[:, h * HEAD_DIM:(h + 1) * HEAD_DIM] = (norm_rope(qh, qg_ref[...]) * scale).astype(BF16)
    for h in range(N_KV_HEADS):
        kh = proj(_OFF_K + h * HEAD_DIM, HEAD_DIM)
        k_ref[:, h * HEAD_DIM:(h + 1) * HEAD_DIM] = norm_rope(kh, kg_ref[...]).astype(BF16)


def _inproj(u, P, cos_t, sin_t, *, tm=256):
    b, L, _ = u.shape

    def tok(width):
        return pl.BlockSpec((None, tm, width), lambda bi, i: (bi, i, 0))

    pos = pl.BlockSpec((tm, HEAD_DIM), lambda bi, i: (i, 0))
    outs = [(D_INNER, F32), (CONV_DIM, F32), (ATTN_WIDTH, BF16), (KV_WIDTH, BF16), (KV_WIDTH, BF16),
            (2 * D_MODEL, F32), (DT_PAD, F32)]
    out_shape = [jax.ShapeDtypeStruct((b, L, w), dt) for w, dt in outs]
    out_specs = [tok(w) for w, _ in outs]
    out_shape.append(jax.ShapeDtypeStruct((b, DT_PAD, L), F32))
    out_specs.append(pl.BlockSpec((None, DT_PAD, tm), lambda bi, i: (bi, 0, i)))
    return pl.pallas_call(
        _inproj_kernel,
        grid=(b, L // tm),
        in_specs=[tok(D_MODEL), _resident((D_MODEL, _IN_COLS)), _resident((DT_PAD, D_MODEL)), pos, pos,
                  _resident((1, HEAD_DIM)), _resident((1, HEAD_DIM)), _resident((1, 2 * D_MODEL)),
                  _resident((1, DT_PAD)), _resident((DT_PAD, 1))],
        out_specs=out_specs,
        out_shape=out_shape,
        compiler_params=_params("parallel", "parallel"),
        name="in_proj",
    )(u, P["w_in"], P["w_dt_t"], cos_t, sin_t, P["q_gain"], P["k_gain"], P["b_gate"],
      P["dt_bias_row"], P["dt_bias_col"])


def _conv_kernel(x_ref, prev_ref, next_ref, w_ref, b_ref, o_ref, ext_ref, *, tl):
    i = pl.program_id(1)
    n = pl.num_programs(1)
    halo = SUBLANES
    ext_ref[0:halo, :] = jnp.where(i > 0, prev_ref[...], 0.0)
    ext_ref[halo:halo + tl, :] = x_ref[...]
    ext_ref[halo + tl:halo + tl + halo, :] = jnp.where(i < n - 1, next_ref[...], 0.0)
    pad = D_CONV // 2
    acc = jnp.zeros((tl, CONV_DIM), F32) + b_ref[...]
    for j in range(D_CONV):
        acc = acc + ext_ref[halo - pad + j:halo - pad + j + tl, :] * w_ref[j:j + 1, :]
    o_ref[...] = _silu(acc)


def _conv(xbc, conv_w, conv_b, *, tl=256):
    b, L, _ = xbc.shape
    r = tl // SUBLANES
    last = L // SUBLANES - 1
    return pl.pallas_call(
        functools.partial(_conv_kernel, tl=tl),
        grid=(b, L // tl),
        in_specs=[pl.BlockSpec((None, tl, CONV_DIM), lambda bi, i: (bi, i, 0)),
                  pl.BlockSpec((None, SUBLANES, CONV_DIM), lambda bi, i: (bi, jnp.maximum(i * r - 1, 0), 0)),
                  pl.BlockSpec((None, SUBLANES, CONV_DIM), lambda bi, i: (bi, jnp.minimum((i + 1) * r, last), 0)),
                  _resident((SUBLANES, CONV_DIM)), _resident((1, CONV_DIM))],
        out_specs=pl.BlockSpec((None, tl, CONV_DIM), lambda bi, i: (bi, i, 0)),
        out_shape=jax.ShapeDtypeStruct((b, L, CONV_DIM), F32),
        scratch_shapes=[pltpu.VMEM((tl + 2 * SUBLANES, CONV_DIM), F32)],
        compiler_params=_params("parallel", "parallel"),
        name="conv",
    )(xbc, xbc, xbc, conv_w, conv_b)


def _expand_heads(v):
    rows = v.shape[0]
    lane = lax.broadcasted_iota(jnp.int32, (rows, LANES), 1)
    parts = []
    for j in range(SSM_HEADS // 2):
        a = jnp.broadcast_to(v[:, 2 * j:2 * j + 1], (rows, LANES))
        bb = jnp.broadcast_to(v[:, 2 * j + 1:2 * j + 2], (rows, LANES))
        parts.append(jnp.where(lane < SSM_HEAD_DIM, a, bb))
    return jnp.concatenate(parts, axis=1)


def _ssd_chunk(xs, bm, cm, dt, dt_t, a_row, a_col, state_ref, first, rev):
    hi = lax.Precision.HIGHEST
    row_i = lax.broadcasted_iota(jnp.int32, (CHUNK, CHUNK), 0)
    col_i = lax.broadcasted_iota(jnp.int32, (CHUNK, CHUNK), 1)
    tri = (col_i >= row_i) if rev else (col_i <= row_i)
    tri_f = tri.astype(F32)
    a = dt * a_row
    a_t = dt_t * a_col
    acum = jnp.dot(tri_f, a, precision=hi, preferred_element_type=F32)
    acum_t = lax.dot_general(a_t, tri_f, (((1,), (1,)), ((), ())), precision=hi,
                             preferred_element_type=F32)
    end = 0 if rev else CHUNK - 1
    total = acum[end:end + 1, :]
    e_rep = _expand_heads(jnp.exp(acum))
    w_rep = _expand_heads(jnp.exp(total - acum) * dt)
    cd_rep = _expand_heads(jnp.exp(total))

    @pl.when(first)
    def _():
        state_ref[...] = jnp.zeros_like(state_ref)

    lane = lax.broadcasted_iota(jnp.int32, (CHUNK, LANES), 1)
    lo_half = lane < SSM_HEAD_DIM
    ys = []
    for g in range(SSM_GROUPS):
        b_g = bm[:, g * D_STATE:(g + 1) * D_STATE].astype(BF16)
        c_g = cm[:, g * D_STATE:(g + 1) * D_STATE].astype(BF16)
        cb = lax.dot_general(c_g, b_g, (((1,), (1,)), ((), ())), preferred_element_type=F32)
        cols = slice(g * GROUP_WIDTH, (g + 1) * GROUP_WIDTH)
        prev = state_ref[g]
        y_off = jnp.dot(c_g, prev.astype(BF16), preferred_element_type=F32) * e_rep[:, cols]
        xs_g = xs[:, cols]
        s_new = lax.dot_general(b_g, (xs_g * w_rep[:, cols]).astype(BF16), (((0,), (0,)), ((), ())),
                                preferred_element_type=F32)
        state_ref[g] = prev * cd_rep[:, cols] + s_new
        pairs = []
        for j in range(GROUP_WIDTH // LANES):
            ms = []
            for hh in range(2):
                h = g * (SSM_HEADS // SSM_GROUPS) + 2 * j + hh
                seg = acum[:, h:h + 1] - acum_t[h:h + 1, :]
                lmat = jnp.exp(jnp.where(tri, seg, -jnp.inf))
                ms.append(cb * lmat * dt_t[h:h + 1, :])
            lhs = jnp.concatenate(ms, axis=1).astype(BF16)
            xp = xs_g[:, j * LANES:(j + 1) * LANES]
            rhs = jnp.concatenate([jnp.where(lo_half, xp, 0.0), jnp.where(lo_half, 0.0, xp)],
                                  axis=0).astype(BF16)
            pairs.append(jnp.dot(lhs, rhs, preferred_element_type=F32))
        ys.append(jnp.concatenate(pairs, axis=1) + y_off)
    return jnp.concatenate(ys, axis=1)


def _ssd_fwd_kernel(xc_ref, dt_ref, dtt_ref, alog_row_ref, alog_col_ref, y_ref, state_ref):
    xc = xc_ref[...]
    y_ref[...] = _ssd_chunk(
        xc[:, :D_INNER], xc[:, D_INNER:D_INNER + BC_WIDTH], xc[:, D_INNER + BC_WIDTH:],
        dt_ref[:, 0:SSM_HEADS], dtt_ref[0:SSM_HEADS, :],
        -jnp.exp(alog_row_ref[:, 0:SSM_HEADS]), -jnp.exp(alog_col_ref[0:SSM_HEADS, :]),
        state_ref, pl.program_id(1) == 0, rev=False)


def _ssd_bwd_kernel(xc_ref, dt_ref, dtt_ref, alog_row_ref, alog_col_ref, yf_ref, z_ref, dskip_ref,
                    gain_ref, s_ref, state_ref):
    xc = xc_ref[...]
    xs = xc[:, :D_INNER]
    y_b = _ssd_chunk(
        xs, xc[:, D_INNER:D_INNER + BC_WIDTH], xc[:, D_INNER + BC_WIDTH:],
        dt_ref[:, SSM_HEADS:2 * SSM_HEADS], dtt_ref[SSM_HEADS:2 * SSM_HEADS, :],
        -jnp.exp(alog_row_ref[:, SSM_HEADS:2 * SSM_HEADS]), -jnp.exp(alog_col_ref[SSM_HEADS:2 * SSM_HEADS, :]),
        state_ref, pl.program_id(1) == 0, rev=True)
    y = (yf_ref[...] + y_b + dskip_ref[...] * xs) * _silu(z_ref[...])
    for g in range(SSM_GROUPS):
        cols = slice(g * GROUP_WIDTH, (g + 1) * GROUP_WIDTH)
        s_ref[:, cols] = _rms(y[:, cols], gain_ref[:, cols]).astype(s_ref.dtype)


def _ssd(xc, dt, dt_t, z, P):
    b, L, _ = xc.shape
    nc = L // CHUNK

    def tok(width, rev):
        if rev:
            return pl.BlockSpec((None, CHUNK, width), lambda bi, c: (bi, nc - 1 - c, 0))
        return pl.BlockSpec((None, CHUNK, width), lambda bi, c: (bi, c, 0))

    def tok_t(rev):
        if rev:
            return pl.BlockSpec((None, DT_PAD, CHUNK), lambda bi, c: (bi, 0, nc - 1 - c))
        return pl.BlockSpec((None, DT_PAD, CHUNK), lambda bi, c: (bi, 0, c))

    state = pltpu.VMEM((SSM_GROUPS, D_STATE, GROUP_WIDTH), F32)
    y_f = pl.pallas_call(
        _ssd_fwd_kernel,
        grid=(b, nc),
        in_specs=[tok(CONV_DIM, False), tok(DT_PAD, False), tok_t(False),
                  _resident((1, DT_PAD)), _resident((DT_PAD, 1))],
        out_specs=tok(D_INNER, False),
        out_shape=jax.ShapeDtypeStruct((b, L, D_INNER), F32),
        scratch_shapes=[state],
        compiler_params=_params("parallel", "arbitrary"),
        name="ssd_fwd",
    )(xc, dt, dt_t, P["a_log_row"], P["a_log_col"])
    return pl.pallas_call(
        _ssd_bwd_kernel,
        grid=(b, nc),
        in_specs=[tok(CONV_DIM, True), tok(DT_PAD, True), tok_t(True),
                  _resident((1, DT_PAD)), _resident((DT_PAD, 1)),
                  tok(D_INNER, True), tok(D_INNER, True),
                  _resident((1, D_INNER)), _resident((1, D_INNER))],
        out_specs=tok(D_INNER, True),
        out_shape=jax.ShapeDtypeStruct((b, L, D_INNER), BF16),
        scratch_shapes=[state],
        compiler_params=_params("parallel", "arbitrary"),
        name="ssd_bwd",
    )(xc, dt, dt_t, P["a_log_row"], P["a_log_col"], y_f, z, P["d_skip_rep"], P["ssm_norm"])


def _attn_kernel(q_ref, k_ref, v_ref, o_ref, m_ref, l_ref, acc_ref):
    kv = pl.program_id(3)

    @pl.when(kv == 0)
    def _():
        m_ref[...] = jnp.full_like(m_ref, -jnp.inf)
        l_ref[...] = jnp.zeros_like(l_ref)
        acc_ref[...] = jnp.zeros_like(acc_ref)

    k = k_ref[...]
    v = v_ref[...]
    for g in range(Q_PER_KV):
        q = q_ref[:, g * HEAD_DIM:(g + 1) * HEAD_DIM]
        s = lax.dot_general(q, k, (((1,), (1,)), ((), ())), preferred_element_type=F32)
        m_prev = m_ref[g]
        m_new = jnp.maximum(m_prev, jnp.max(s, axis=-1, keepdims=True))
        alpha = jnp.exp(m_prev - m_new)
        p = jnp.exp(s - m_new)
        l_ref[g] = alpha * l_ref[g] + jnp.sum(p, axis=-1, keepdims=True)
        acc_ref[g] = alpha * acc_ref[g] + jnp.dot(p.astype(BF16), v, preferred_element_type=F32)
        m_ref[g] = m_new

    @pl.when(kv == pl.num_programs(3) - 1)
    def _():
        for g in range(Q_PER_KV):
            o_ref[:, g * HEAD_DIM:(g + 1) * HEAD_DIM] = (acc_ref[g] / l_ref[g]).astype(o_ref.dtype)


def _attention(q, k, v, *, tq=512, tk=512):
    b, L, _ = q.shape
    gw = Q_PER_KV * HEAD_DIM
    return pl.pallas_call(
        _attn_kernel,
        grid=(b, N_KV_HEADS, L // tq, L // tk),
        in_specs=[pl.BlockSpec((None, tq, gw), lambda bi, h, i, j: (bi, i, h)),
                  pl.BlockSpec((None, tk, HEAD_DIM), lambda bi, h, i, j: (bi, j, h)),
                  pl.BlockSpec((None, tk, HEAD_DIM), lambda bi, h, i, j: (bi, j, h))],
        out_specs=pl.BlockSpec((None, tq, gw), lambda bi, h, i, j: (bi, i, h)),
        out_shape=jax.ShapeDtypeStruct((b, L, ATTN_WIDTH), BF16),
        scratch_shapes=[pltpu.VMEM((Q_PER_KV, tq, 1), F32), pltpu.VMEM((Q_PER_KV, tq, 1), F32),
                        pltpu.VMEM((Q_PER_KV, tq, HEAD_DIM), F32)],
        compiler_params=_params("parallel", "parallel", "parallel", "arbitrary"),
        name="attn",
    )(q, k, v)


def _merge_kernel(s_ref, a_ref, g_ref, h_ref, ws_ref, wa_ref, wo_ref, o_ref):
    ps = jnp.dot(s_ref[...], ws_ref[...], preferred_element_type=F32)
    pa = jnp.dot(a_ref[...], wa_ref[...], preferred_element_type=F32)
    m = g_ref[:, :D_MODEL] * ps + g_ref[:, D_MODEL:] * pa
    o_ref[...] = h_ref[...] + jnp.dot(m.astype(BF16), wo_ref[...], preferred_element_type=F32)


def _merge(s, a, g, h, P, *, tm=512):
    b, L, _ = h.shape

    def tok(width):
        return pl.BlockSpec((None, tm, width), lambda bi, i: (bi, i, 0))

    return pl.pallas_call(
        _merge_kernel,
        grid=(b, L // tm),
        in_specs=[tok(D_INNER), tok(ATTN_WIDTH), tok(2 * D_MODEL), tok(D_MODEL),
                  _resident((D_INNER, D_MODEL)), _resident((ATTN_WIDTH, D_MODEL)),
                  _resident((D_MODEL, D_MODEL))],
        out_specs=tok(D_MODEL),
        out_shape=jax.ShapeDtypeStruct((b, L, D_MODEL), F32),
        compiler_params=_params("parallel", "parallel"),
        name="merge",
    )(s, a, g, h, P["w_ssm_branch"], P["w_attn_branch"], P["w_out"])


def _rope_tables(L):
    rows = L // GRID_W
    row = jnp.repeat(jnp.arange(rows, dtype=F32), GRID_W)
    col = jnp.tile(jnp.arange(GRID_W, dtype=F32), rows)
    axis_dim = HEAD_DIM // 2
    inv_freq = ROPE_THETA ** (-jnp.arange(0, axis_dim, 2, dtype=F32) / axis_dim)
    ang = jnp.concatenate([row[:, None] * inv_freq, col[:, None] * inv_freq], axis=-1)
    cos, sin = jnp.cos(ang), jnp.sin(ang)
    return jnp.concatenate([cos, cos], axis=-1), jnp.concatenate([-sin, sin], axis=-1)


def _layer_params(norm_ffn1, w_ffn1_gu, w_ffn1_down, norm_mix, w_in, conv_w, conv_b, dt_bias_f, dt_bias_b,
                  A_log_f, A_log_b, D_skip, ssm_norm, q_norm, k_norm, w_ssm_branch, w_attn_branch, b_gate,
                  w_out, norm_ffn2, w_ffn2_gu, w_ffn2_down):
    sizes = (D_INNER, CONV_DIM, SSM_HEADS, SSM_HEADS, ATTN_WIDTH, KV_WIDTH, KV_WIDTH, 2 * D_MODEL)
    splits = [int(v) for v in np.cumsum(sizes)[:-1]]
    w_z, w_xbc, w_dtf, w_dtb, w_q, w_k, w_v, w_g = jnp.split(w_in, splits, axis=-1)
    perm = np.concatenate([np.arange(0, HEAD_DIM, 2), np.arange(1, HEAD_DIM, 2)])
    q_cols = np.concatenate([h * HEAD_DIM + perm for h in range(N_Q_HEADS)])
    k_cols = np.concatenate([h * HEAD_DIM + perm for h in range(N_KV_HEADS)])
    w_dt = jnp.concatenate([w_dtf, w_dtb, jnp.zeros((D_MODEL, DT_PAD - 2 * SSM_HEADS), F32)], axis=-1)
    w_r = jnp.concatenate([w_z, w_xbc, w_q[:, q_cols], w_k[:, k_cols], w_v, w_g, w_dt], axis=-1)
    pad = jnp.zeros((DT_PAD - 2 * SSM_HEADS,), F32)
    dt_bias = jnp.concatenate([dt_bias_f, dt_bias_b, pad])
    a_log = jnp.concatenate([A_log_f, A_log_b, pad])
    row = lambda v: v.reshape(1, -1).astype(F32)
    return dict(
        norm_ffn1=row(norm_ffn1), w_ffn1_gu=w_ffn1_gu.astype(BF16), w_ffn1_down=w_ffn1_down.astype(BF16),
        norm_mix=row(norm_mix), w_in=w_r.astype(BF16), w_dt_t=w_dt.T.astype(BF16),
        conv_w=jnp.concatenate([conv_w, jnp.zeros((SUBLANES - D_CONV, CONV_DIM), F32)], axis=0),
        conv_b=row(conv_b), dt_bias_row=row(dt_bias), dt_bias_col=dt_bias.reshape(-1, 1),
        a_log_row=row(a_log), a_log_col=a_log.reshape(-1, 1),
        d_skip_rep=row(jnp.repeat(D_skip, SSM_HEAD_DIM)), ssm_norm=row(ssm_norm),
        q_gain=row(q_norm[perm]), k_gain=row(k_norm[perm]),
        w_ssm_branch=w_ssm_branch.astype(BF16), w_attn_branch=w_attn_branch.astype(BF16),
        b_gate=row(b_gate), w_out=w_out.astype(BF16),
        norm_ffn2=row(norm_ffn2), w_ffn2_gu=w_ffn2_gu.astype(BF16), w_ffn2_down=w_ffn2_down.astype(BF16),
    )


def _encoder_layer(x, P, g_next, last):
    L = x.shape[1]
    h, u = _ffn(x, P["norm_ffn1"], P["w_ffn1_gu"], P["w_ffn1_down"], P["norm_mix"],
                emit_h=True, post_dtype=BF16)
    cos_t, sin_t = _rope_tables(L)
    z, xbc, q, k, v, g, dt, dt_t = _inproj(u, P, cos_t, sin_t)
    xc = _conv(xbc, P["conv_w"], P["conv_b"])
    s = _ssd(xc, dt, dt_t, z, P)
    a = _attention(q, k, v)
    h2 = _merge(s, a, g, h, P)
    outs = _ffn(h2, P["norm_ffn2"], P["w_ffn2_gu"], P["w_ffn2_down"], g_next,
                emit_h=not last, post_dtype=F32)
    return outs


def _trunk(x, layers, norm_final):
    depth = len(layers)
    for i, P in enumerate(layers):
        last = i == depth - 1
        g_next = norm_final.reshape(1, -1) if last else layers[i + 1]["norm_ffn1"]
        outs = _encoder_layer(x, P, g_next, last)
        x = outs[0]
    return x


def kernel(x_prompt, x_sample, norm_ffn1, w_ffn1_gu, w_ffn1_down, norm_mix, w_in, conv_w, conv_b, dt_bias_f,
           dt_bias_b, A_log_f, A_log_b, D_skip, ssm_norm, q_norm, k_norm, w_ssm_branch, w_attn_branch, b_gate,
           w_out, norm_ffn2, w_ffn2_gu, w_ffn2_down, norm_final):
    per_layer = (norm_ffn1, w_ffn1_gu, w_ffn1_down, norm_mix, w_in, conv_w, conv_b, dt_bias_f, dt_bias_b,
                 A_log_f, A_log_b, D_skip, ssm_norm, q_norm, k_norm, w_ssm_branch, w_attn_branch, b_gate,
                 w_out, norm_ffn2, w_ffn2_gu, w_ffn2_down)
    depth = norm_ffn1.shape[0]
    layers = [_layer_params(*(w[i] for w in per_layer)) for i in range(depth)]
    return (_trunk(x_prompt, layers, norm_final), _trunk(x_sample, layers, norm_final))
```

```python
import functools

import jax
import jax.numpy as jnp
import numpy as np
from jax import lax
from jax.experimental import pallas as pl
from jax.experimental.pallas import tpu as pltpu

F32 = jnp.float32
BF16 = jnp.bfloat16

D_MODEL = 1024
GRID_W = 64
D_INNER = 2048
SSM_HEAD_DIM = 64
SSM_HEADS = 32
SSM_GROUPS = 4
GROUP_WIDTH = D_INNER // SSM_GROUPS
D_STATE = 128
D_CONV = 5
CHUNK = 128
BC_WIDTH = SSM_GROUPS * D_STATE
CONV_DIM = D_INNER + 2 * BC_WIDTH
HEAD_DIM = 128
N_Q_HEADS = 8
N_KV_HEADS = 2
Q_PER_KV = N_Q_HEADS // N_KV_HEADS
ATTN_WIDTH = N_Q_HEADS * HEAD_DIM
KV_WIDTH = N_KV_HEADS * HEAD_DIM
ROPE_THETA = 10000.0
D_FF = 2816
EPS = 1e-6
DT_PAD = 128

LANES = 128
SUBLANES = 8
VMEM_LIMIT = 56 * 1024 * 1024

LOG2_E = 1.4426950408889634

_OFF_Z = 0
_OFF_XBC = _OFF_Z + D_INNER
_OFF_K = _OFF_XBC + CONV_DIM
_OFF_G = _OFF_K + KV_WIDTH
_OFF_DT = _OFF_G + 2 * D_MODEL
_IN_COLS = _OFF_DT + DT_PAD
_ROW_Q = 0
_ROW_V = _ROW_Q + ATTN_WIDTH
_ROW_DT = _ROW_V + KV_WIDTH
_T_ROWS = _ROW_DT + DT_PAD


def _rms(x, g):
    return x * lax.rsqrt(jnp.mean(x * x, axis=-1, keepdims=True) + EPS) * g


def _silu(x):
    return x * jax.nn.sigmoid(x)


def _softplus(x):
    return jnp.maximum(x, 0.0) + jnp.log1p(jnp.exp(-jnp.abs(x)))


def _params(*sem):
    return pltpu.CompilerParams(dimension_semantics=sem, vmem_limit_bytes=VMEM_LIMIT)


def _resident(shape):
    nd = len(shape)
    return pl.BlockSpec(shape, lambda *_: (0,) * nd, pipeline_mode=pl.Buffered(1))


def _ffn_kernel(x_ref, gpre_ref, wgu_ref, wd_ref, gpost_ref, *out_refs, tf, emit_h):
    x = x_ref[...]
    xn = _rms(x, gpre_ref[...]).astype(BF16)
    acc = jnp.zeros_like(x)
    for c in range(D_FF // tf):
        g = jnp.dot(xn, wgu_ref[:, c * tf:(c + 1) * tf], preferred_element_type=F32)
        u = jnp.dot(xn, wgu_ref[:, D_FF + c * tf:D_FF + (c + 1) * tf], preferred_element_type=F32)
        act = (_silu(g) * u).astype(BF16)
        acc = acc + jnp.dot(act, wd_ref[c * tf:(c + 1) * tf, :], preferred_element_type=F32)
    h = x + 0.5 * acc
    if emit_h:
        out_refs[0][...] = h
    post_ref = out_refs[-1]
    post_ref[...] = _rms(h, gpost_ref[...]).astype(post_ref.dtype)


def _ffn(x, g_pre, w_gu, w_down, g_post, *, emit_h, post_dtype, tm=512, tf=256):
    b, L, _ = x.shape
    tok = pl.BlockSpec((None, tm, D_MODEL), lambda bi, i: (bi, i, 0))
    out_shape = [jax.ShapeDtypeStruct((b, L, D_MODEL), post_dtype)]
    out_specs = [tok]
    if emit_h:
        out_shape.insert(0, jax.ShapeDtypeStruct((b, L, D_MODEL), F32))
        out_specs.insert(0, tok)
    return pl.pallas_call(
        functools.partial(_ffn_kernel, tf=tf, emit_h=emit_h),
        grid=(b, L // tm),
        in_specs=[tok, _resident((1, D_MODEL)), _resident((D_MODEL, 2 * D_FF)),
                  _resident((D_FF, D_MODEL)), _resident((1, D_MODEL))],
        out_specs=out_specs,
        out_shape=out_shape,
        compiler_params=_params("parallel", "parallel"),
        name="ffn",
    )(x, g_pre, w_gu, w_down, g_post)


def _inproj_kernel(u_ref, w_ref, wt_ref, cos_ref, sin_ref, cos_t_ref, sin_t_ref, qg_ref, kg_ref, bg_ref,
                   dtb_row_ref, dtb_col_ref,
                   z_ref, xbc_ref, qt_ref, k_ref, vt_ref, g_ref, dt_ref, dtt_ref):
    u = u_ref[...]

    def proj(off, width):
        return jnp.dot(u, w_ref[:, off:off + width], preferred_element_type=F32)

    def proj_t(off, rows):
        return lax.dot_general(wt_ref[off:off + rows, :], u, (((1,), (1,)), ((), ())),
                               preferred_element_type=F32)

    z_ref[...] = proj(_OFF_Z, D_INNER)
    xbc_ref[...] = proj(_OFF_XBC, CONV_DIM)
    g_ref[...] = jax.nn.sigmoid(proj(_OFF_G, 2 * D_MODEL) + bg_ref[...])
    dt_ref[...] = _softplus(proj(_OFF_DT, DT_PAD) + dtb_row_ref[...])
    dtt_ref[...] = _softplus(proj_t(_ROW_DT, DT_PAD) + dtb_col_ref[...])
    vt_ref[...] = proj_t(_ROW_V, KV_WIDTH).astype(BF16)

    half = HEAD_DIM // 2
    cos = cos_ref[...]
    sin = sin_ref[...]
    for h in range(N_KV_HEADS):
        x = _rms(proj(_OFF_K + h * HEAD_DIM, HEAD_DIM), kg_ref[...])
        k_ref[:, h * HEAD_DIM:(h + 1) * HEAD_DIM] = (x * cos + pltpu.roll(x, half, axis=1) * sin).astype(BF16)

    cos_t = cos_t_ref[...]
    sin_t = sin_t_ref[...]
    q_scale = HEAD_DIM ** -0.5 * LOG2_E
    for h in range(N_Q_HEADS):
        x = proj_t(_ROW_Q + h * HEAD_DIM, HEAD_DIM)
        x = x * lax.rsqrt(jnp.mean(x * x, axis=0, keepdims=True) + EPS) * qg_ref[...]
        rot = jnp.concatenate([x[half:], x[:half]], axis=0)
        qt_ref[h * HEAD_DIM:(h + 1) * HEAD_DIM, :] = ((x * cos_t + rot * sin_t) * q_scale).astype(BF16)


def _inproj(u, P, tables, *, tm=256):
    b, L, _ = u.shape

    def tok(width):
        return pl.BlockSpec((None, tm, width), lambda bi, i: (bi, i, 0))

    def tok_t(rows):
        return pl.BlockSpec((None, rows, tm), lambda bi, i: (bi, 0, i))

    pos = pl.BlockSpec((tm, HEAD_DIM), lambda bi, i: (i, 0))
    pos_t = pl.BlockSpec((HEAD_DIM, tm), lambda bi, i: (0, i))
    outs = [(tok, D_INNER, F32), (tok, CONV_DIM, F32), (tok_t, ATTN_WIDTH, BF16), (tok, KV_WIDTH, BF16),
            (tok_t, KV_WIDTH, BF16), (tok, 2 * D_MODEL, F32), (tok, DT_PAD, F32), (tok_t, DT_PAD, F32)]
    out_shape = [jax.ShapeDtypeStruct((b, L, w) if mk is tok else (b, w, L), dt) for mk, w, dt in outs]
    out_specs = [mk(w) for mk, w, _ in outs]
    cos, sin, cos_t, sin_t = tables
    return pl.pallas_call(
        _inproj_kernel,
        grid=(b, L // tm),
        in_specs=[tok(D_MODEL), _resident((D_MODEL, _IN_COLS)), _resident((_T_ROWS, D_MODEL)), pos, pos,
                  pos_t, pos_t, _resident((HEAD_DIM, 1)), _resident((1, HEAD_DIM)), _resident((1, 2 * D_MODEL)),
                  _resident((1, DT_PAD)), _resident((DT_PAD, 1))],
        out_specs=out_specs,
        out_shape=out_shape,
        compiler_params=_params("parallel", "parallel"),
        name="in_proj",
    )(u, P["w_in"], P["w_in_t"], cos, sin, cos_t, sin_t, P["q_gain_col"], P["k_gain"], P["b_gate"],
      P["dt_bias_row"], P["dt_bias_col"])


def _conv_kernel(x_ref, prev_ref, next_ref, w_ref, b_ref, o_ref, ext_ref, *, tl):
    i = pl.program_id(1)
    n = pl.num_programs(1)
    halo = SUBLANES
    ext_ref[0:halo, :] = jnp.where(i > 0, prev_ref[...], 0.0)
    ext_ref[halo:halo + tl, :] = x_ref[...]
    ext_ref[halo + tl:halo + tl + halo, :] = jnp.where(i < n - 1, next_ref[...], 0.0)
    pad = D_CONV // 2
    acc = jnp.zeros((tl, CONV_DIM), F32) + b_ref[...]
    for j in range(D_CONV):
        acc = acc + ext_ref[halo - pad + j:halo - pad + j + tl, :] * w_ref[j:j + 1, :]
    o_ref[...] = _silu(acc)


def _conv(xbc, conv_w, conv_b, *, tl=256):
    b, L, _ = xbc.shape
    r = tl // SUBLANES
    last = L // SUBLANES - 1
    return pl.pallas_call(
        functools.partial(_conv_kernel, tl=tl),
        grid=(b, L // tl),
        in_specs=[pl.BlockSpec((None, tl, CONV_DIM), lambda bi, i: (bi, i, 0)),
                  pl.BlockSpec((None, SUBLANES, CONV_DIM), lambda bi, i: (bi, jnp.maximum(i * r - 1, 0), 0)),
                  pl.BlockSpec((None, SUBLANES, CONV_DIM), lambda bi, i: (bi, jnp.minimum((i + 1) * r, last), 0)),
                  _resident((SUBLANES, CONV_DIM)), _resident((1, CONV_DIM))],
        out_specs=pl.BlockSpec((None, tl, CONV_DIM), lambda bi, i: (bi, i, 0)),
        out_shape=jax.ShapeDtypeStruct((b, L, CONV_DIM), F32),
        scratch_shapes=[pltpu.VMEM((tl + 2 * SUBLANES, CONV_DIM), F32)],
        compiler_params=_params("parallel", "parallel"),
        name="conv",
    )(xbc, xbc, xbc, conv_w, conv_b)


def _expand_heads(v):
    rows = v.shape[0]
    lane = lax.broadcasted_iota(jnp.int32, (rows, LANES), 1)
    parts = []
    for j in range(SSM_HEADS // 2):
        a = jnp.broadcast_to(v[:, 2 * j:2 * j + 1], (rows, LANES))
        bb = jnp.broadcast_to(v[:, 2 * j + 1:2 * j + 2], (rows, LANES))
        parts.append(jnp.where(lane < SSM_HEAD_DIM, a, bb))
    return jnp.concatenate(parts, axis=1)


def _ssd_chunk(xs, bm, cm, dt, dt_t, a_row, a_col, state_ref, first, rev):
    hi = lax.Precision.HIGHEST
    row_i = lax.broadcasted_iota(jnp.int32, (CHUNK, CHUNK), 0)
    col_i = lax.broadcasted_iota(jnp.int32, (CHUNK, CHUNK), 1)
    tri = (col_i >= row_i) if rev else (col_i <= row_i)
    tri_f = tri.astype(F32)
    a = dt * a_row
    a_t = dt_t * a_col
    acum = jnp.dot(tri_f, a, precision=hi, preferred_element_type=F32)
    acum_t = lax.dot_general(a_t, tri_f, (((1,), (1,)), ((), ())), precision=hi,
                             preferred_element_type=F32)
    end = 0 if rev else CHUNK - 1
    total = acum[end:end + 1, :]
    e_rep = _expand_heads(jnp.exp(acum))
    w_rep = _expand_heads(jnp.exp(total - acum) * dt)
    cd_rep = _expand_heads(jnp.exp(total))

    @pl.when(first)
    def _():
        state_ref[...] = jnp.zeros_like(state_ref)

    lane = lax.broadcasted_iota(jnp.int32, (CHUNK, LANES), 1)
    lo_half = lane < SSM_HEAD_DIM
    ys = []
    for g in range(SSM_GROUPS):
        b_g = bm[:, g * D_STATE:(g + 1) * D_STATE].astype(BF16)
        c_g = cm[:, g * D_STATE:(g + 1) * D_STATE].astype(BF16)
        cb = lax.dot_general(c_g, b_g, (((1,), (1,)), ((), ())), preferred_element_type=F32)
        cols = slice(g * GROUP_WIDTH, (g + 1) * GROUP_WIDTH)
        prev = state_ref[g]
        y_off = jnp.dot(c_g, prev.astype(BF16), preferred_element_type=F32) * e_rep[:, cols]
        xs_g = xs[:, cols]
        s_new = lax.dot_general(b_g, (xs_g * w_rep[:, cols]).astype(BF16), (((0,), (0,)), ((), ())),
                                preferred_element_type=F32)
        state_ref[g] = prev * cd_rep[:, cols] + s_new
        pairs = []
        for j in range(GROUP_WIDTH // LANES):
            ms = []
            for hh in range(2):
                h = g * (SSM_HEADS // SSM_GROUPS) + 2 * j + hh
                seg = acum[:, h:h + 1] - acum_t[h:h + 1, :]
                lmat = jnp.exp(jnp.where(tri, seg, -jnp.inf))
                ms.append(cb * lmat * dt_t[h:h + 1, :])
            lhs = jnp.concatenate(ms, axis=1).astype(BF16)
            xp = xs_g[:, j * LANES:(j + 1) * LANES]
            rhs = jnp.concatenate([jnp.where(lo_half, xp, 0.0), jnp.where(lo_half, 0.0, xp)],
                                  axis=0).astype(BF16)
            pairs.append(jnp.dot(lhs, rhs, preferred_element_type=F32))
        ys.append(jnp.concatenate(pairs, axis=1) + y_off)
    return jnp.concatenate(ys, axis=1)


def _ssd_fwd_kernel(xc_ref, dt_ref, dtt_ref, alog_row_ref, alog_col_ref, y_ref, state_ref):
    xc = xc_ref[...]
    y_ref[...] = _ssd_chunk(
        xc[:, :D_INNER], xc[:, D_INNER:D_INNER + BC_WIDTH], xc[:, D_INNER + BC_WIDTH:],
        dt_ref[:, 0:SSM_HEADS], dtt_ref[0:SSM_HEADS, :],
        -jnp.exp(alog_row_ref[:, 0:SSM_HEADS]), -jnp.exp(alog_col_ref[0:SSM_HEADS, :]),
        state_ref, pl.program_id(1) == 0, rev=False)


def _ssd_bwd_kernel(xc_ref, dt_ref, dtt_ref, alog_row_ref, alog_col_ref, yf_ref, z_ref, dskip_ref,
                    gain_ref, s_ref, state_ref):
    xc = xc_ref[...]
    xs = xc[:, :D_INNER]
    y_b = _ssd_chunk(
        xs, xc[:, D_INNER:D_INNER + BC_WIDTH], xc[:, D_INNER + BC_WIDTH:],
        dt_ref[:, SSM_HEADS:2 * SSM_HEADS], dtt_ref[SSM_HEADS:2 * SSM_HEADS, :],
        -jnp.exp(alog_row_ref[:, SSM_HEADS:2 * SSM_HEADS]), -jnp.exp(alog_col_ref[SSM_HEADS:2 * SSM_HEADS, :]),
        state_ref, pl.program_id(1) == 0, rev=True)
    y = (yf_ref[...] + y_b + dskip_ref[...] * xs) * _silu(z_ref[...])
    for g in range(SSM_GROUPS):
        cols = slice(g * GROUP_WIDTH, (g + 1) * GROUP_WIDTH)
        s_ref[:, cols] = _rms(y[:, cols], gain_ref[:, cols]).astype(s_ref.dtype)


def _ssd(xc, dt, dt_t, z, P):
    b, L, _ = xc.shape
    nc = L // CHUNK

    def tok(width, rev):
        if rev:
            return pl.BlockSpec((None, CHUNK, width), lambda bi, c: (bi, nc - 1 - c, 0))
        return pl.BlockSpec((None, CHUNK, width), lambda bi, c: (bi, c, 0))

    def tok_t(rev):
        if rev:
            return pl.BlockSpec((None, DT_PAD, CHUNK), lambda bi, c: (bi, 0, nc - 1 - c))
        return pl.BlockSpec((None, DT_PAD, CHUNK), lambda bi, c: (bi, 0, c))

    state = pltpu.VMEM((SSM_GROUPS, D_STATE, GROUP_WIDTH), F32)
    y_f = pl.pallas_call(
        _ssd_fwd_kernel,
        grid=(b, nc),
        in_specs=[tok(CONV_DIM, False), tok(DT_PAD, False), tok_t(False),
                  _resident((1, DT_PAD)), _resident((DT_PAD, 1))],
        out_specs=tok(D_INNER, False),
        out_shape=jax.ShapeDtypeStruct((b, L, D_INNER), F32),
        scratch_shapes=[state],
        compiler_params=_params("parallel", "arbitrary"),
        name="ssd_fwd",
    )(xc, dt, dt_t, P["a_log_row"], P["a_log_col"])
    return pl.pallas_call(
        _ssd_bwd_kernel,
        grid=(b, nc),
        in_specs=[tok(CONV_DIM, True), tok(DT_PAD, True), tok_t(True),
                  _resident((1, DT_PAD)), _resident((DT_PAD, 1)),
                  tok(D_INNER, True), tok(D_INNER, True),
                  _resident((1, D_INNER)), _resident((1, D_INNER))],
        out_specs=tok(D_INNER, True),
        out_shape=jax.ShapeDtypeStruct((b, L, D_INNER), BF16),
        scratch_shapes=[state],
        compiler_params=_params("parallel", "arbitrary"),
        name="ssd_bwd",
    )(xc, dt, dt_t, P["a_log_row"], P["a_log_col"], y_f, z, P["d_skip_rep"], P["ssm_norm"])


ONES_ROWS = 16


def _attn_kernel(qt_ref, k_ref, vt_ref, o_ref, m_ref, acc_ref, *, tq, nb):
    kv = pl.program_id(3)

    @pl.when(kv == 0)
    def _():
        m_ref[...] = jnp.full_like(m_ref, -jnp.inf)
        acc_ref[...] = jnp.zeros_like(acc_ref)

    k = k_ref[...]
    vt = vt_ref[...]
    vt_aug = jnp.concatenate([vt, jnp.ones((ONES_ROWS, vt.shape[1]), BF16)], axis=0)
    blocks = [(g, c) for g in range(Q_PER_KV) for c in range(tq // nb)]

    def scores(g, c):
        qt = qt_ref[g * HEAD_DIM:(g + 1) * HEAD_DIM, c * nb:(c + 1) * nb]
        return jnp.dot(k, qt, preferred_element_type=F32)

    s_next = scores(*blocks[0])
    for i, (g, c) in enumerate(blocks):
        s = s_next
        if i + 1 < len(blocks):
            s_next = scores(*blocks[i + 1])
        cols = slice(g * tq + c * nb, g * tq + (c + 1) * nb)
        m_prev = m_ref[:, cols]
        m_new = jnp.maximum(m_prev, jnp.max(s, axis=0, keepdims=True))
        alpha = jnp.exp2(m_prev - m_new)
        p = jnp.exp2(s - m_new).astype(BF16)
        acc_ref[:, cols] = alpha * acc_ref[:, cols] + jnp.dot(vt_aug, p, preferred_element_type=F32)
        m_ref[:, cols] = m_new

    @pl.when(kv == pl.num_programs(3) - 1)
    def _():
        for g in range(Q_PER_KV):
            acc = acc_ref[:, g * tq:(g + 1) * tq]
            out_t = acc[:HEAD_DIM] / acc[HEAD_DIM:HEAD_DIM + 1]
            o_ref[:, g * HEAD_DIM:(g + 1) * HEAD_DIM] = out_t.T.astype(o_ref.dtype)


def _attention(qt, k, vt, *, tq=512, tk=512, nb=256):
    b, L, _ = k.shape
    gw = Q_PER_KV * HEAD_DIM
    return pl.pallas_call(
        functools.partial(_attn_kernel, tq=tq, nb=nb),
        grid=(b, N_KV_HEADS, L // tq, L // tk),
        in_specs=[pl.BlockSpec((None, gw, tq), lambda bi, h, i, j: (bi, h, i)),
                  pl.BlockSpec((None, tk, HEAD_DIM), lambda bi, h, i, j: (bi, j, h)),
                  pl.BlockSpec((None, HEAD_DIM, tk), lambda bi, h, i, j: (bi, h, j))],
        out_specs=pl.BlockSpec((None, tq, gw), lambda bi, h, i, j: (bi, i, h)),
        out_shape=jax.ShapeDtypeStruct((b, L, ATTN_WIDTH), BF16),
        scratch_shapes=[pltpu.VMEM((1, Q_PER_KV * tq), F32),
                        pltpu.VMEM((HEAD_DIM + ONES_ROWS, Q_PER_KV * tq), F32)],
        compiler_params=_params("parallel", "parallel", "parallel", "arbitrary"),
        name="attn",
    )(qt, k, vt)


def _merge_kernel(s_ref, a_ref, g_ref, h_ref, ws_ref, wa_ref, wo_ref, o_ref):
    ps = jnp.dot(s_ref[...], ws_ref[...], preferred_element_type=F32)
    pa = jnp.dot(a_ref[...], wa_ref[...], preferred_element_type=F32)
    m = g_ref[:, :D_MODEL] * ps + g_ref[:, D_MODEL:] * pa
    o_ref[...] = h_ref[...] + jnp.dot(m.astype(BF16), wo_ref[...], preferred_element_type=F32)


def _merge(s, a, g, h, P, *, tm=512):
    b, L, _ = h.shape

    def tok(width):
        return pl.BlockSpec((None, tm, width), lambda bi, i: (bi, i, 0))

    return pl.pallas_call(
        _merge_kernel,
        grid=(b, L // tm),
        in_specs=[tok(D_INNER), tok(ATTN_WIDTH), tok(2 * D_MODEL), tok(D_MODEL),
                  _resident((D_INNER, D_MODEL)), _resident((ATTN_WIDTH, D_MODEL)),
                  _resident((D_MODEL, D_MODEL))],
        out_specs=tok(D_MODEL),
        out_shape=jax.ShapeDtypeStruct((b, L, D_MODEL), F32),
        compiler_params=_params("parallel", "parallel"),
        name="merge",
    )(s, a, g, h, P["w_ssm_branch"], P["w_attn_branch"], P["w_out"])


def _rope_tables(L):
    rows = L // GRID_W
    row = jnp.repeat(jnp.arange(rows, dtype=F32), GRID_W)
    col = jnp.tile(jnp.arange(GRID_W, dtype=F32), rows)
    axis_dim = HEAD_DIM // 2
    inv_freq = ROPE_THETA ** (-jnp.arange(0, axis_dim, 2, dtype=F32) / axis_dim)
    ang = jnp.concatenate([row[:, None] * inv_freq, col[:, None] * inv_freq], axis=-1)
    cos, sin = jnp.cos(ang), jnp.sin(ang)
    cos2, sin2 = jnp.concatenate([cos, cos], axis=-1), jnp.concatenate([-sin, sin], axis=-1)
    return cos2, sin2, cos2.T, sin2.T


def _layer_params(norm_ffn1, w_ffn1_gu, w_ffn1_down, norm_mix, w_in, conv_w, conv_b, dt_bias_f, dt_bias_b,
                  A_log_f, A_log_b, D_skip, ssm_norm, q_norm, k_norm, w_ssm_branch, w_attn_branch, b_gate,
                  w_out, norm_ffn2, w_ffn2_gu, w_ffn2_down):
    sizes = (D_INNER, CONV_DIM, SSM_HEADS, SSM_HEADS, ATTN_WIDTH, KV_WIDTH, KV_WIDTH, 2 * D_MODEL)
    splits = [int(v) for v in np.cumsum(sizes)[:-1]]
    w_z, w_xbc, w_dtf, w_dtb, w_q, w_k, w_v, w_g = jnp.split(w_in, splits, axis=-1)
    perm = np.concatenate([np.arange(0, HEAD_DIM, 2), np.arange(1, HEAD_DIM, 2)])
    q_cols = np.concatenate([h * HEAD_DIM + perm for h in range(N_Q_HEADS)])
    k_cols = np.concatenate([h * HEAD_DIM + perm for h in range(N_KV_HEADS)])
    w_dt = jnp.concatenate([w_dtf, w_dtb, jnp.zeros((D_MODEL, DT_PAD - 2 * SSM_HEADS), F32)], axis=-1)
    w_r = jnp.concatenate([w_z, w_xbc, w_k[:, k_cols], w_g, w_dt], axis=-1)
    w_t = jnp.concatenate([w_q[:, q_cols], w_v, w_dt], axis=-1).T
    pad = jnp.zeros((DT_PAD - 2 * SSM_HEADS,), F32)
    dt_bias = jnp.concatenate([dt_bias_f, dt_bias_b, pad])
    a_log = jnp.concatenate([A_log_f, A_log_b, pad])
    row = lambda v: v.reshape(1, -1).astype(F32)
    return dict(
        norm_ffn1=row(norm_ffn1), w_ffn1_gu=w_ffn1_gu.astype(BF16), w_ffn1_down=w_ffn1_down.astype(BF16),
        norm_mix=row(norm_mix), w_in=w_r.astype(BF16), w_in_t=w_t.astype(BF16),
        conv_w=jnp.concatenate([conv_w, jnp.zeros((SUBLANES - D_CONV, CONV_DIM), F32)], axis=0),
        conv_b=row(conv_b), dt_bias_row=row(dt_bias), dt_bias_col=dt_bias.reshape(-1, 1),
        a_log_row=row(a_log), a_log_col=a_log.reshape(-1, 1),
        d_skip_rep=row(jnp.repeat(D_skip, SSM_HEAD_DIM)), ssm_norm=row(ssm_norm),
        q_gain_col=q_norm[perm].reshape(-1, 1).astype(F32), k_gain=row(k_norm[perm]),
        w_ssm_branch=w_ssm_branch.astype(BF16), w_attn_branch=w_attn_branch.astype(BF16),
        b_gate=row(b_gate), w_out=w_out.astype(BF16),
        norm_ffn2=row(norm_ffn2), w_ffn2_gu=w_ffn2_gu.astype(BF16), w_ffn2_down=w_ffn2_down.astype(BF16),
    )


def _encoder_layer(x, P, g_next, last):
    L = x.shape[1]
    h, u = _ffn(x, P["norm_ffn1"], P["w_ffn1_gu"], P["w_ffn1_down"], P["norm_mix"],
                emit_h=True, post_dtype=BF16)
    z, xbc, q_t, k, v_t, g, dt, dt_t = _inproj(u, P, _rope_tables(L))
    xc = _conv(xbc, P["conv_w"], P["conv_b"])
    s = _ssd(xc, dt, dt_t, z, P)
    a = _attention(q_t, k, v_t)
    h2 = _merge(s, a, g, h, P)
    outs = _ffn(h2, P["norm_ffn2"], P["w_ffn2_gu"], P["w_ffn2_down"], g_next,
                emit_h=not last, post_dtype=F32)
    return outs


def _trunk(x, layers, norm_final):
    depth = len(layers)
    for i, P in enumerate(layers):
        last = i == depth - 1
        g_next = norm_final.reshape(1, -1) if last else layers[i + 1]["norm_ffn1"]
        outs = _encoder_layer(x, P, g_next, last)
        x = outs[0]
    return x


def kernel(x_prompt, x_sample, norm_ffn1, w_ffn1_gu, w_ffn1_down, norm_mix, w_in, conv_w, conv_b, dt_bias_f,
           dt_bias_b, A_log_f, A_log_b, D_skip, ssm_norm, q_norm, k_norm, w_ssm_branch, w_attn_branch, b_gate,
           w_out, norm_ffn2, w_ffn2_gu, w_ffn2_down, norm_final):
    per_layer = (norm_ffn1, w_ffn1_gu, w_ffn1_down, norm_mix, w_in, conv_w, conv_b, dt_bias_f, dt_bias_b,
                 A_log_f, A_log_b, D_skip, ssm_norm, q_norm, k_norm, w_ssm_branch, w_attn_branch, b_gate,
                 w_out, norm_ffn2, w_ffn2_gu, w_ffn2_down)
    depth = norm_ffn1.shape[0]
    layers = [_layer_params(*(w[i] for w in per_layer)) for i in range(depth)]
    return (_trunk(x_prompt, layers, norm_final), _trunk(x_sample, layers, norm_final))
```

```python
import functools

import jax
import jax.numpy as jnp
import numpy as np
from jax import lax
from jax.experimental import pallas as pl
from jax.experimental.pallas import tpu as pltpu

F32 = jnp.float32
BF16 = jnp.bfloat16

D_MODEL = 1024
GRID_W = 64
D_INNER = 2048
SSM_HEAD_DIM = 64
SSM_HEADS = 32
SSM_GROUPS = 4
GROUP_WIDTH = D_INNER // SSM_GROUPS
D_STATE = 128
D_CONV = 5
CHUNK = 128
BC_WIDTH = SSM_GROUPS * D_STATE
CONV_DIM = D_INNER + 2 * BC_WIDTH
HEAD_DIM = 128
N_Q_HEADS = 8
N_KV_HEADS = 2
Q_PER_KV = N_Q_HEADS // N_KV_HEADS
ATTN_WIDTH = N_Q_HEADS * HEAD_DIM
KV_WIDTH = N_KV_HEADS * HEAD_DIM
ROPE_THETA = 10000.0
D_FF = 2816
EPS = 1e-6
DT_PAD = 128

LANES = 128
SUBLANES = 8
VMEM_LIMIT = 56 * 1024 * 1024

LOG2_E = 1.4426950408889634

_OFF_Z = 0
_OFF_XBC = _OFF_Z + D_INNER
_OFF_K = _OFF_XBC + CONV_DIM
_OFF_G = _OFF_K + KV_WIDTH
_OFF_DT = _OFF_G + 2 * D_MODEL
_IN_COLS = _OFF_DT + DT_PAD
_ROW_Q = 0
_ROW_V = _ROW_Q + ATTN_WIDTH
_ROW_DT = _ROW_V + KV_WIDTH
_T_ROWS = _ROW_DT + DT_PAD


def _rms(x, g):
    return x * lax.rsqrt(jnp.mean(x * x, axis=-1, keepdims=True) + EPS) * g


def _silu(x):
    return x * jax.nn.sigmoid(x)


def _softplus(x):
    return jnp.maximum(x, 0.0) + jnp.log1p(jnp.exp(-jnp.abs(x)))


def _params(*sem):
    return pltpu.CompilerParams(dimension_semantics=sem, vmem_limit_bytes=VMEM_LIMIT)


def _resident(shape):
    nd = len(shape)
    return pl.BlockSpec(shape, lambda *_: (0,) * nd, pipeline_mode=pl.Buffered(1))


def _ffn_kernel(x_ref, gpre_ref, wgu_ref, wd_ref, gpost_ref, *out_refs, tf, emit_h):
    x = x_ref[...]
    xn = _rms(x, gpre_ref[...]).astype(BF16)
    acc = jnp.zeros_like(x)
    for c in range(D_FF // tf):
        g = jnp.dot(xn, wgu_ref[:, c * tf:(c + 1) * tf], preferred_element_type=F32)
        u = jnp.dot(xn, wgu_ref[:, D_FF + c * tf:D_FF + (c + 1) * tf], preferred_element_type=F32)
        act = (_silu(g) * u).astype(BF16)
        acc = acc + jnp.dot(act, wd_ref[c * tf:(c + 1) * tf, :], preferred_element_type=F32)
    h = x + 0.5 * acc
    if emit_h:
        out_refs[0][...] = h
    post_ref = out_refs[-1]
    post_ref[...] = _rms(h, gpost_ref[...]).astype(post_ref.dtype)


def _ffn(x, g_pre, w_gu, w_down, g_post, *, emit_h, post_dtype, tm=512, tf=256):
    b, L, _ = x.shape
    tok = pl.BlockSpec((None, tm, D_MODEL), lambda bi, i: (bi, i, 0))
    out_shape = [jax.ShapeDtypeStruct((b, L, D_MODEL), post_dtype)]
    out_specs = [tok]
    if emit_h:
        out_shape.insert(0, jax.ShapeDtypeStruct((b, L, D_MODEL), F32))
        out_specs.insert(0, tok)
    return pl.pallas_call(
        functools.partial(_ffn_kernel, tf=tf, emit_h=emit_h),
        grid=(b, L // tm),
        in_specs=[tok, _resident((1, D_MODEL)), _resident((D_MODEL, 2 * D_FF)),
                  _resident((D_FF, D_MODEL)), _resident((1, D_MODEL))],
        out_specs=out_specs,
        out_shape=out_shape,
        compiler_params=_params("parallel", "parallel"),
        name="ffn",
    )(x, g_pre, w_gu, w_down, g_post)


def _inproj_kernel(u_ref, w_ref, wt_ref, cos_ref, sin_ref, cos_t_ref, sin_t_ref, qg_ref, kg_ref, bg_ref,
                   dtb_row_ref, dtb_col_ref,
                   z_ref, xbc_ref, qt_ref, k_ref, vt_ref, g_ref, dt_ref, dtt_ref):
    u = u_ref[...]

    def proj(off, width):
        return jnp.dot(u, w_ref[:, off:off + width], preferred_element_type=F32)

    def proj_t(off, rows):
        return lax.dot_general(wt_ref[off:off + rows, :], u, (((1,), (1,)), ((), ())),
                               preferred_element_type=F32)

    z_ref[...] = proj(_OFF_Z, D_INNER)
    xbc_ref[...] = proj(_OFF_XBC, CONV_DIM)
    g_ref[...] = jax.nn.sigmoid(proj(_OFF_G, 2 * D_MODEL) + bg_ref[...])
    dt_ref[...] = _softplus(proj(_OFF_DT, DT_PAD) + dtb_row_ref[...])
    dtt_ref[...] = _softplus(proj_t(_ROW_DT, DT_PAD) + dtb_col_ref[...])
    vt_ref[...] = proj_t(_ROW_V, KV_WIDTH).astype(BF16)

    half = HEAD_DIM // 2
    cos = cos_ref[...]
    sin = sin_ref[...]
    for h in range(N_KV_HEADS):
        x = _rms(proj(_OFF_K + h * HEAD_DIM, HEAD_DIM), kg_ref[...])
        k_ref[:, h * HEAD_DIM:(h + 1) * HEAD_DIM] = (x * cos + pltpu.roll(x, half, axis=1) * sin).astype(BF16)

    cos_t = cos_t_ref[...]
    sin_t = sin_t_ref[...]
    q_scale = HEAD_DIM ** -0.5 * LOG2_E
    for h in range(N_Q_HEADS):
        x = proj_t(_ROW_Q + h * HEAD_DIM, HEAD_DIM)
        x = x * lax.rsqrt(jnp.mean(x * x, axis=0, keepdims=True) + EPS) * qg_ref[...]
        rot = jnp.concatenate([x[half:], x[:half]], axis=0)
        qt_ref[h * HEAD_DIM:(h + 1) * HEAD_DIM, :] = ((x * cos_t + rot * sin_t) * q_scale).astype(BF16)


def _inproj(u, P, tables, *, tm=256):
    b, L, _ = u.shape

    def tok(width):
        return pl.BlockSpec((None, tm, width), lambda bi, i: (bi, i, 0))

    def tok_t(rows):
        return pl.BlockSpec((None, rows, tm), lambda bi, i: (bi, 0, i))

    pos = pl.BlockSpec((tm, HEAD_DIM), lambda bi, i: (i, 0))
    pos_t = pl.BlockSpec((HEAD_DIM, tm), lambda bi, i: (0, i))
    outs = [(tok, D_INNER, F32), (tok, CONV_DIM, F32), (tok_t, ATTN_WIDTH, BF16), (tok, KV_WIDTH, BF16),
            (tok_t, KV_WIDTH, BF16), (tok, 2 * D_MODEL, F32), (tok, DT_PAD, F32), (tok_t, DT_PAD, F32)]
    out_shape = [jax.ShapeDtypeStruct((b, L, w) if mk is tok else (b, w, L), dt) for mk, w, dt in outs]
    out_specs = [mk(w) for mk, w, _ in outs]
    cos, sin, cos_t, sin_t = tables
    return pl.pallas_call(
        _inproj_kernel,
        grid=(b, L // tm),
        in_specs=[tok(D_MODEL), _resident((D_MODEL, _IN_COLS)), _resident((_T_ROWS, D_MODEL)), pos, pos,
                  pos_t, pos_t, _resident((HEAD_DIM, 1)), _resident((1, HEAD_DIM)), _resident((1, 2 * D_MODEL)),
                  _resident((1, DT_PAD)), _resident((DT_PAD, 1))],
        out_specs=out_specs,
        out_shape=out_shape,
        compiler_params=_params("parallel", "parallel"),
        name="in_proj",
    )(u, P["w_in"], P["w_in_t"], cos, sin, cos_t, sin_t, P["q_gain_col"], P["k_gain"], P["b_gate"],
      P["dt_bias_row"], P["dt_bias_col"])


def _conv_kernel(x_ref, prev_ref, next_ref, w_ref, b_ref, o_ref, ext_ref, *, tl):
    i = pl.program_id(1)
    n = pl.num_programs(1)
    halo = SUBLANES
    ext_ref[0:halo, :] = jnp.where(i > 0, prev_ref[...], 0.0)
    ext_ref[halo:halo + tl, :] = x_ref[...]
    ext_ref[halo + tl:halo + tl + halo, :] = jnp.where(i < n - 1, next_ref[...], 0.0)
    pad = D_CONV // 2
    acc = jnp.zeros((tl, CONV_DIM), F32) + b_ref[...]
    for j in range(D_CONV):
        acc = acc + ext_ref[halo - pad + j:halo - pad + j + tl, :] * w_ref[j:j + 1, :]
    o_ref[...] = _silu(acc)


def _conv(xbc, conv_w, conv_b, *, tl=256):
    b, L, _ = xbc.shape
    r = tl // SUBLANES
    last = L // SUBLANES - 1
    return pl.pallas_call(
        functools.partial(_conv_kernel, tl=tl),
        grid=(b, L // tl),
        in_specs=[pl.BlockSpec((None, tl, CONV_DIM), lambda bi, i: (bi, i, 0)),
                  pl.BlockSpec((None, SUBLANES, CONV_DIM), lambda bi, i: (bi, jnp.maximum(i * r - 1, 0), 0)),
                  pl.BlockSpec((None, SUBLANES, CONV_DIM), lambda bi, i: (bi, jnp.minimum((i + 1) * r, last), 0)),
                  _resident((SUBLANES, CONV_DIM)), _resident((1, CONV_DIM))],
        out_specs=pl.BlockSpec((None, tl, CONV_DIM), lambda bi, i: (bi, i, 0)),
        out_shape=jax.ShapeDtypeStruct((b, L, CONV_DIM), F32),
        scratch_shapes=[pltpu.VMEM((tl + 2 * SUBLANES, CONV_DIM), F32)],
        compiler_params=_params("parallel", "parallel"),
        name="conv",
    )(xbc, xbc, xbc, conv_w, conv_b)


def _expand_heads(v):
    rows = v.shape[0]
    lane = lax.broadcasted_iota(jnp.int32, (rows, LANES), 1)
    parts = []
    for j in range(SSM_HEADS // 2):
        a = jnp.broadcast_to(v[:, 2 * j:2 * j + 1], (rows, LANES))
        bb = jnp.broadcast_to(v[:, 2 * j + 1:2 * j + 2], (rows, LANES))
        parts.append(jnp.where(lane < SSM_HEAD_DIM, a, bb))
    return jnp.concatenate(parts, axis=1)


def _ssd_chunk(xs, bm, cm, dt, dt_t, a_row, a_col, state_ref, first, rev):
    hi = lax.Precision.HIGHEST
    row_i = lax.broadcasted_iota(jnp.int32, (CHUNK, CHUNK), 0)
    col_i = lax.broadcasted_iota(jnp.int32, (CHUNK, CHUNK), 1)
    tri = (col_i >= row_i) if rev else (col_i <= row_i)
    tri_f = tri.astype(F32)
    a = dt * a_row
    a_t = dt_t * a_col
    acum = jnp.dot(tri_f, a, precision=hi, preferred_element_type=F32)
    acum_t = lax.dot_general(a_t, tri_f, (((1,), (1,)), ((), ())), precision=hi,
                             preferred_element_type=F32)
    end = 0 if rev else CHUNK - 1
    total = acum[end:end + 1, :]
    e_rep = _expand_heads(jnp.exp(acum))
    w_rep = _expand_heads(jnp.exp(total - acum) * dt)
    cd_rep = _expand_heads(jnp.exp(total))

    @pl.when(first)
    def _():
        state_ref[...] = jnp.zeros_like(state_ref)

    lane = lax.broadcasted_iota(jnp.int32, (CHUNK, LANES), 1)
    lo_half = lane < SSM_HEAD_DIM
    ys = []
    for g in range(SSM_GROUPS):
        b_g = bm[:, g * D_STATE:(g + 1) * D_STATE].astype(BF16)
        c_g = cm[:, g * D_STATE:(g + 1) * D_STATE].astype(BF16)
        cb = lax.dot_general(c_g, b_g, (((1,), (1,)), ((), ())), preferred_element_type=F32)
        cols = slice(g * GROUP_WIDTH, (g + 1) * GROUP_WIDTH)
        prev = state_ref[g]
        y_off = jnp.dot(c_g, prev.astype(BF16), preferred_element_type=F32) * e_rep[:, cols]
        xs_g = xs[:, cols]
        s_new = lax.dot_general(b_g, (xs_g * w_rep[:, cols]).astype(BF16), (((0,), (0,)), ((), ())),
                                preferred_element_type=F32)
        state_ref[g] = prev * cd_rep[:, cols] + s_new
        pairs = []
        for j in range(GROUP_WIDTH // LANES):
            ms = []
            for hh in range(2):
                h = g * (SSM_HEADS // SSM_GROUPS) + 2 * j + hh
                seg = acum[:, h:h + 1] - acum_t[h:h + 1, :]
                lmat = jnp.exp(jnp.where(tri, seg, -jnp.inf))
                ms.append(cb * lmat * dt_t[h:h + 1, :])
            lhs = jnp.concatenate(ms, axis=1).astype(BF16)
            xp = xs_g[:, j * LANES:(j + 1) * LANES]
            rhs = jnp.concatenate([jnp.where(lo_half, xp, 0.0), jnp.where(lo_half, 0.0, xp)],
                                  axis=0).astype(BF16)
            pairs.append(jnp.dot(lhs, rhs, preferred_element_type=F32))
        ys.append(jnp.concatenate(pairs, axis=1) + y_off)
    return jnp.concatenate(ys, axis=1)


def _ssd_fwd_kernel(xc_ref, dt_ref, dtt_ref, alog_row_ref, alog_col_ref, y_ref, state_ref):
    xc = xc_ref[...]
    y_ref[...] = _ssd_chunk(
        xc[:, :D_INNER], xc[:, D_INNER:D_INNER + BC_WIDTH], xc[:, D_INNER + BC_WIDTH:],
        dt_ref[:, 0:SSM_HEADS], dtt_ref[0:SSM_HEADS, :],
        -jnp.exp(alog_row_ref[:, 0:SSM_HEADS]), -jnp.exp(alog_col_ref[0:SSM_HEADS, :]),
        state_ref, pl.program_id(1) == 0, rev=False)


def _ssd_bwd_kernel(xc_ref, dt_ref, dtt_ref, alog_row_ref, alog_col_ref, yf_ref, z_ref, dskip_ref,
                    gain_ref, s_ref, state_ref):
    xc = xc_ref[...]
    xs = xc[:, :D_INNER]
    y_b = _ssd_chunk(
        xs, xc[:, D_INNER:D_INNER + BC_WIDTH], xc[:, D_INNER + BC_WIDTH:],
        dt_ref[:, SSM_HEADS:2 * SSM_HEADS], dtt_ref[SSM_HEADS:2 * SSM_HEADS, :],
        -jnp.exp(alog_row_ref[:, SSM_HEADS:2 * SSM_HEADS]), -jnp.exp(alog_col_ref[SSM_HEADS:2 * SSM_HEADS, :]),
        state_ref, pl.program_id(1) == 0, rev=True)
    y = (yf_ref[...] + y_b + dskip_ref[...] * xs) * _silu(z_ref[...])
    for g in range(SSM_GROUPS):
        cols = slice(g * GROUP_WIDTH, (g + 1) * GROUP_WIDTH)
        s_ref[:, cols] = _rms(y[:, cols], gain_ref[:, cols]).astype(s_ref.dtype)


def _ssd(xc, dt, dt_t, z, P):
    b, L, _ = xc.shape
    nc = L // CHUNK

    def tok(width, rev):
        if rev:
            return pl.BlockSpec((None, CHUNK, width), lambda bi, c: (bi, nc - 1 - c, 0))
        return pl.BlockSpec((None, CHUNK, width), lambda bi, c: (bi, c, 0))

    def tok_t(rev):
        if rev:
            return pl.BlockSpec((None, DT_PAD, CHUNK), lambda bi, c: (bi, 0, nc - 1 - c))
        return pl.BlockSpec((None, DT_PAD, CHUNK), lambda bi, c: (bi, 0, c))

    state = pltpu.VMEM((SSM_GROUPS, D_STATE, GROUP_WIDTH), F32)
    y_f = pl.pallas_call(
        _ssd_fwd_kernel,
        grid=(b, nc),
        in_specs=[tok(CONV_DIM, False), tok(DT_PAD, False), tok_t(False),
                  _resident((1, DT_PAD)), _resident((DT_PAD, 1))],
        out_specs=tok(D_INNER, False),
        out_shape=jax.ShapeDtypeStruct((b, L, D_INNER), F32),
        scratch_shapes=[state],
        compiler_params=_params("parallel", "arbitrary"),
        name="ssd_fwd",
    )(xc, dt, dt_t, P["a_log_row"], P["a_log_col"])
    return pl.pallas_call(
        _ssd_bwd_kernel,
        grid=(b, nc),
        in_specs=[tok(CONV_DIM, True), tok(DT_PAD, True), tok_t(True),
                  _resident((1, DT_PAD)), _resident((DT_PAD, 1)),
                  tok(D_INNER, True), tok(D_INNER, True),
                  _resident((1, D_INNER)), _resident((1, D_INNER))],
        out_specs=tok(D_INNER, True),
        out_shape=jax.ShapeDtypeStruct((b, L, D_INNER), BF16),
        scratch_shapes=[state],
        compiler_params=_params("parallel", "arbitrary"),
        name="ssd_bwd",
    )(xc, dt, dt_t, P["a_log_row"], P["a_log_col"], y_f, z, P["d_skip_rep"], P["ssm_norm"])


ONES_ROWS = 16
QK_LOOKAHEAD = 3


def _attn_kernel(qt_ref, k_ref, vt_ref, o_ref, m_ref, acc_ref, *, tq, nb):
    kv = pl.program_id(3)

    @pl.when(kv == 0)
    def _():
        m_ref[...] = jnp.full_like(m_ref, -jnp.inf)
        acc_ref[...] = jnp.zeros_like(acc_ref)

    k = k_ref[...]
    vt = vt_ref[...]
    vt_aug = jnp.concatenate([vt, jnp.ones((ONES_ROWS, vt.shape[1]), BF16)], axis=0)
    blocks = [(g, c) for g in range(Q_PER_KV) for c in range(tq // nb)]

    def scores(g, c):
        qt = qt_ref[g * HEAD_DIM:(g + 1) * HEAD_DIM, c * nb:(c + 1) * nb]
        return jnp.dot(k, qt, preferred_element_type=F32)

    pending = [scores(*blk) for blk in blocks[:QK_LOOKAHEAD]]
    for i, (g, c) in enumerate(blocks):
        s = pending.pop(0)
        if i + QK_LOOKAHEAD < len(blocks):
            pending.append(scores(*blocks[i + QK_LOOKAHEAD]))
        cols = slice(g * tq + c * nb, g * tq + (c + 1) * nb)
        m_prev = m_ref[:, cols]
        m_new = jnp.maximum(m_prev, jnp.max(s, axis=0, keepdims=True))
        alpha = jnp.exp2(m_prev - m_new)
        p = jnp.exp2(s - m_new).astype(BF16)
        acc_ref[:, cols] = alpha * acc_ref[:, cols] + jnp.dot(vt_aug, p, preferred_element_type=F32)
        m_ref[:, cols] = m_new

    @pl.when(kv == pl.num_programs(3) - 1)
    def _():
        for g in range(Q_PER_KV):
            acc = acc_ref[:, g * tq:(g + 1) * tq]
            out_t = acc[:HEAD_DIM] / acc[HEAD_DIM:HEAD_DIM + 1]
            o_ref[:, g * HEAD_DIM:(g + 1) * HEAD_DIM] = out_t.T.astype(o_ref.dtype)


def _attn_bounded_kernel(qt_ref, k_ref, vt_ref, o_ref, acc_ref, *, tq, nb):
    kv = pl.program_id(3)

    @pl.when(kv == 0)
    def _():
        acc_ref[...] = jnp.zeros_like(acc_ref)

    k = k_ref[...]
    vt = vt_ref[...]
    vt_aug = jnp.concatenate([vt, jnp.ones((ONES_ROWS, vt.shape[1]), BF16)], axis=0)
    blocks = [(g, c) for g in range(Q_PER_KV) for c in range(tq // nb)]

    def scores(g, c):
        qt = qt_ref[g * HEAD_DIM:(g + 1) * HEAD_DIM, c * nb:(c + 1) * nb]
        return jnp.dot(k, qt, preferred_element_type=F32)

    pending = [scores(*blk) for blk in blocks[:QK_LOOKAHEAD]]
    for i, (g, c) in enumerate(blocks):
        s = pending.pop(0)
        if i + QK_LOOKAHEAD < len(blocks):
            pending.append(scores(*blocks[i + QK_LOOKAHEAD]))
        cols = slice(g * tq + c * nb, g * tq + (c + 1) * nb)
        acc_ref[:, cols] += jnp.dot(vt_aug, jnp.exp2(s).astype(BF16), preferred_element_type=F32)

    @pl.when(kv == pl.num_programs(3) - 1)
    def _():
        for g in range(Q_PER_KV):
            acc = acc_ref[:, g * tq:(g + 1) * tq]
            out_t = acc[:HEAD_DIM] / acc[HEAD_DIM:HEAD_DIM + 1]
            o_ref[:, g * HEAD_DIM:(g + 1) * HEAD_DIM] = out_t.T.astype(o_ref.dtype)


def _attention(qt, k, vt, *, bounded, tq, tk=512, nb=256):
    b, L, _ = k.shape
    gw = Q_PER_KV * HEAD_DIM
    acc = pltpu.VMEM((HEAD_DIM + ONES_ROWS, Q_PER_KV * tq), F32)
    if bounded:
        body, scratch = _attn_bounded_kernel, [acc]
    else:
        body, scratch = _attn_kernel, [pltpu.VMEM((1, Q_PER_KV * tq), F32), acc]
    return pl.pallas_call(
        functools.partial(body, tq=tq, nb=nb),
        grid=(b, N_KV_HEADS, L // tq, L // tk),
        in_specs=[pl.BlockSpec((None, gw, tq), lambda bi, h, i, j: (bi, h, i)),
                  pl.BlockSpec((None, tk, HEAD_DIM), lambda bi, h, i, j: (bi, j, h)),
                  pl.BlockSpec((None, HEAD_DIM, tk), lambda bi, h, i, j: (bi, h, j))],
        out_specs=pl.BlockSpec((None, tq, gw), lambda bi, h, i, j: (bi, i, h)),
        out_shape=jax.ShapeDtypeStruct((b, L, ATTN_WIDTH), BF16),
        scratch_shapes=scratch,
        compiler_params=_params("parallel", "parallel", "parallel", "arbitrary"),
        name="attn_bounded" if bounded else "attn",
    )(qt, k, vt)


SCORE_BOUND = 40.0
_BF16_NORM_SLACK = (1.0 + 2.0 ** -8) ** 2


def _attention_dispatch(qt, k, vt, q_gain, k_gain):
    L = k.shape[1]
    bound = (LOG2_E * HEAD_DIM ** 0.5 * _BF16_NORM_SLACK) * jnp.max(jnp.abs(q_gain)) * jnp.max(jnp.abs(k_gain))
    tq_fast = 1024 if L % 1024 == 0 else 512
    return lax.cond(bound <= SCORE_BOUND,
                    functools.partial(_attention, bounded=True, tq=tq_fast),
                    functools.partial(_attention, bounded=False, tq=512),
                    qt, k, vt)


def _merge_kernel(s_ref, a_ref, g_ref, h_ref, ws_ref, wa_ref, wo_ref, o_ref):
    ps = jnp.dot(s_ref[...], ws_ref[...], preferred_element_type=F32)
    pa = jnp.dot(a_ref[...], wa_ref[...], preferred_element_type=F32)
    m = g_ref[:, :D_MODEL] * ps + g_ref[:, D_MODEL:] * pa
    o_ref[...] = h_ref[...] + jnp.dot(m.astype(BF16), wo_ref[...], preferred_element_type=F32)


def _merge(s, a, g, h, P, *, tm=512):
    b, L, _ = h.shape

    def tok(width):
        return pl.BlockSpec((None, tm, width), lambda bi, i: (bi, i, 0))

    return pl.pallas_call(
        _merge_kernel,
        grid=(b, L // tm),
        in_specs=[tok(D_INNER), tok(ATTN_WIDTH), tok(2 * D_MODEL), tok(D_MODEL),
                  _resident((D_INNER, D_MODEL)), _resident((ATTN_WIDTH, D_MODEL)),
                  _resident((D_MODEL, D_MODEL))],
        out_specs=tok(D_MODEL),
        out_shape=jax.ShapeDtypeStruct((b, L, D_MODEL), F32),
        compiler_params=_params("parallel", "parallel"),
        name="merge",
    )(s, a, g, h, P["w_ssm_branch"], P["w_attn_branch"], P["w_out"])


def _rope_tables(L):
    rows = L // GRID_W
    row = jnp.repeat(jnp.arange(rows, dtype=F32), GRID_W)
    col = jnp.tile(jnp.arange(GRID_W, dtype=F32), rows)
    axis_dim = HEAD_DIM // 2
    inv_freq = ROPE_THETA ** (-jnp.arange(0, axis_dim, 2, dtype=F32) / axis_dim)
    ang = jnp.concatenate([row[:, None] * inv_freq, col[:, None] * inv_freq], axis=-1)
    cos, sin = jnp.cos(ang), jnp.sin(ang)
    cos2, sin2 = jnp.concatenate([cos, cos], axis=-1), jnp.concatenate([-sin, sin], axis=-1)
    return cos2, sin2, cos2.T, sin2.T


def _layer_params(norm_ffn1, w_ffn1_gu, w_ffn1_down, norm_mix, w_in, conv_w, conv_b, dt_bias_f, dt_bias_b,
                  A_log_f, A_log_b, D_skip, ssm_norm, q_norm, k_norm, w_ssm_branch, w_attn_branch, b_gate,
                  w_out, norm_ffn2, w_ffn2_gu, w_ffn2_down):
    sizes = (D_INNER, CONV_DIM, SSM_HEADS, SSM_HEADS, ATTN_WIDTH, KV_WIDTH, KV_WIDTH, 2 * D_MODEL)
    splits = [int(v) for v in np.cumsum(sizes)[:-1]]
    w_z, w_xbc, w_dtf, w_dtb, w_q, w_k, w_v, w_g = jnp.split(w_in, splits, axis=-1)
    perm = np.concatenate([np.arange(0, HEAD_DIM, 2), np.arange(1, HEAD_DIM, 2)])
    q_cols = np.concatenate([h * HEAD_DIM + perm for h in range(N_Q_HEADS)])
    k_cols = np.concatenate([h * HEAD_DIM + perm for h in range(N_KV_HEADS)])
    w_dt = jnp.concatenate([w_dtf, w_dtb, jnp.zeros((D_MODEL, DT_PAD - 2 * SSM_HEADS), F32)], axis=-1)
    w_r = jnp.concatenate([w_z, w_xbc, w_k[:, k_cols], w_g, w_dt], axis=-1)
    w_t = jnp.concatenate([w_q[:, q_cols], w_v, w_dt], axis=-1).T
    pad = jnp.zeros((DT_PAD - 2 * SSM_HEADS,), F32)
    dt_bias = jnp.concatenate([dt_bias_f, dt_bias_b, pad])
    a_log = jnp.concatenate([A_log_f, A_log_b, pad])
    row = lambda v: v.reshape(1, -1).astype(F32)
    return dict(
        norm_ffn1=row(norm_ffn1), w_ffn1_gu=w_ffn1_gu.astype(BF16), w_ffn1_down=w_ffn1_down.astype(BF16),
        norm_mix=row(norm_mix), w_in=w_r.astype(BF16), w_in_t=w_t.astype(BF16),
        conv_w=jnp.concatenate([conv_w, jnp.zeros((SUBLANES - D_CONV, CONV_DIM), F32)], axis=0),
        conv_b=row(conv_b), dt_bias_row=row(dt_bias), dt_bias_col=dt_bias.reshape(-1, 1),
        a_log_row=row(a_log), a_log_col=a_log.reshape(-1, 1),
        d_skip_rep=row(jnp.repeat(D_skip, SSM_HEAD_DIM)), ssm_norm=row(ssm_norm),
        q_gain_col=q_norm[perm].reshape(-1, 1).astype(F32), k_gain=row(k_norm[perm]),
        w_ssm_branch=w_ssm_branch.astype(BF16), w_attn_branch=w_attn_branch.astype(BF16),
        b_gate=row(b_gate), w_out=w_out.astype(BF16),
        norm_ffn2=row(norm_ffn2), w_ffn2_gu=w_ffn2_gu.astype(BF16), w_ffn2_down=w_ffn2_down.astype(BF16),
    )


def _encoder_layer(x, P, g_next, last):
    L = x.shape[1]
    h, u = _ffn(x, P["norm_ffn1"], P["w_ffn1_gu"], P["w_ffn1_down"], P["norm_mix"],
                emit_h=True, post_dtype=BF16)
    z, xbc, q_t, k, v_t, g, dt, dt_t = _inproj(u, P, _rope_tables(L))
    xc = _conv(xbc, P["conv_w"], P["conv_b"])
    s = _ssd(xc, dt, dt_t, z, P)
    a = _attention_dispatch(q_t, k, v_t, P["q_gain_col"], P["k_gain"])
    h2 = _merge(s, a, g, h, P)
    outs = _ffn(h2, P["norm_ffn2"], P["w_ffn2_gu"], P["w_ffn2_down"], g_next,
                emit_h=not last, post_dtype=F32)
    return outs


def _trunk(x, layers, norm_final):
    depth = len(layers)
    for i, P in enumerate(layers):
        last = i == depth - 1
        g_next = norm_final.reshape(1, -1) if last else layers[i + 1]["norm_ffn1"]
        outs = _encoder_layer(x, P, g_next, last)
        x = outs[0]
    return x


def kernel(x_prompt, x_sample, norm_ffn1, w_ffn1_gu, w_ffn1_down, norm_mix, w_in, conv_w, conv_b, dt_bias_f,
           dt_bias_b, A_log_f, A_log_b, D_skip, ssm_norm, q_norm, k_norm, w_ssm_branch, w_attn_branch, b_gate,
           w_out, norm_ffn2, w_ffn2_gu, w_ffn2_down, norm_final):
    per_layer = (norm_ffn1, w_ffn1_gu, w_ffn1_down, norm_mix, w_in, conv_w, conv_b, dt_bias_f, dt_bias_b,
                 A_log_f, A_log_b, D_skip, ssm_norm, q_norm, k_norm, w_ssm_branch, w_attn_branch, b_gate,
                 w_out, norm_ffn2, w_ffn2_gu, w_ffn2_down)
    depth = norm_ffn1.shape[0]
    layers = [_layer_params(*(w[i] for w in per_layer)) for i in range(depth)]
    return (_trunk(x_prompt, layers, norm_final), _trunk(x_sample, layers, norm_final))
```

```python
import functools

import jax
import jax.numpy as jnp
import numpy as np
from jax import lax
from jax.experimental import pallas as pl
from jax.experimental.pallas import tpu as pltpu

F32 = jnp.float32
BF16 = jnp.bfloat16

D_MODEL = 1024
GRID_W = 64
D_INNER = 2048
SSM_HEAD_DIM = 64
SSM_HEADS = 32
SSM_GROUPS = 4
GROUP_WIDTH = D_INNER // SSM_GROUPS
D_STATE = 128
D_CONV = 5
CHUNK = 128
BC_WIDTH = SSM_GROUPS * D_STATE
CONV_DIM = D_INNER + 2 * BC_WIDTH
HEAD_DIM = 128
N_Q_HEADS = 8
N_KV_HEADS = 2
Q_PER_KV = N_Q_HEADS // N_KV_HEADS
ATTN_WIDTH = N_Q_HEADS * HEAD_DIM
KV_WIDTH = N_KV_HEADS * HEAD_DIM
ROPE_THETA = 10000.0
D_FF = 2816
EPS = 1e-6
DT_PAD = 128

LANES = 128
SUBLANES = 8
VMEM_LIMIT = 56 * 1024 * 1024

LOG2_E = 1.4426950408889634

_OFF_Z = 0
_OFF_XBC = _OFF_Z + D_INNER
_OFF_K = _OFF_XBC + CONV_DIM
_OFF_G = _OFF_K + KV_WIDTH
_OFF_DT = _OFF_G + 2 * D_MODEL
_IN_COLS = _OFF_DT + DT_PAD
_ROW_Q = 0
_ROW_V = _ROW_Q + ATTN_WIDTH
_ROW_DT = _ROW_V + KV_WIDTH
_T_ROWS = _ROW_DT + DT_PAD


def _rms(x, g):
    return x * lax.rsqrt(jnp.mean(x * x, axis=-1, keepdims=True) + EPS) * g


def _silu(x):
    return x * jax.nn.sigmoid(x)


def _softplus(x):
    return jnp.maximum(x, 0.0) + jnp.log1p(jnp.exp(-jnp.abs(x)))


def _params(*sem):
    return pltpu.CompilerParams(dimension_semantics=sem, vmem_limit_bytes=VMEM_LIMIT)


def _resident(shape):
    nd = len(shape)
    return pl.BlockSpec(shape, lambda *_: (0,) * nd, pipeline_mode=pl.Buffered(1))


def _ffn_kernel(x_ref, gpre_ref, wgu_ref, wd_ref, gpost_ref, *out_refs, tf, emit_h):
    x = x_ref[...]
    xn = _rms(x, gpre_ref[...]).astype(BF16)
    acc = jnp.zeros_like(x)
    for c in range(D_FF // tf):
        g = jnp.dot(xn, wgu_ref[:, c * tf:(c + 1) * tf], preferred_element_type=F32)
        u = jnp.dot(xn, wgu_ref[:, D_FF + c * tf:D_FF + (c + 1) * tf], preferred_element_type=F32)
        act = (_silu(g) * u).astype(BF16)
        acc = acc + jnp.dot(act, wd_ref[c * tf:(c + 1) * tf, :], preferred_element_type=F32)
    h = x + 0.5 * acc
    if emit_h:
        out_refs[0][...] = h
    post_ref = out_refs[-1]
    post_ref[...] = _rms(h, gpost_ref[...]).astype(post_ref.dtype)


def _ffn(x, g_pre, w_gu, w_down, g_post, *, emit_h, post_dtype, tm=512, tf=256):
    b, L, _ = x.shape
    tok = pl.BlockSpec((None, tm, D_MODEL), lambda bi, i: (bi, i, 0))
    out_shape = [jax.ShapeDtypeStruct((b, L, D_MODEL), post_dtype)]
    out_specs = [tok]
    if emit_h:
        out_shape.insert(0, jax.ShapeDtypeStruct((b, L, D_MODEL), F32))
        out_specs.insert(0, tok)
    return pl.pallas_call(
        functools.partial(_ffn_kernel, tf=tf, emit_h=emit_h),
        grid=(b, L // tm),
        in_specs=[tok, _resident((1, D_MODEL)), _resident((D_MODEL, 2 * D_FF)),
                  _resident((D_FF, D_MODEL)), _resident((1, D_MODEL))],
        out_specs=out_specs,
        out_shape=out_shape,
        compiler_params=_params("parallel", "parallel"),
        name="ffn",
    )(x, g_pre, w_gu, w_down, g_post)


def _inproj_kernel(u_ref, w_ref, wt_ref, cos_ref, sin_ref, cos_t_ref, sin_t_ref, qg_ref, kg_ref, bg_ref,
                   dtb_row_ref, dtb_col_ref,
                   z_ref, xbc_ref, qt_ref, k_ref, vt_ref, g_ref, dt_ref, dtt_ref):
    u = u_ref[...]

    def proj(off, width):
        return jnp.dot(u, w_ref[:, off:off + width], preferred_element_type=F32)

    def proj_t(off, rows):
        return lax.dot_general(wt_ref[off:off + rows, :], u, (((1,), (1,)), ((), ())),
                               preferred_element_type=F32)

    z_ref[...] = proj(_OFF_Z, D_INNER)
    xbc_ref[...] = proj(_OFF_XBC, CONV_DIM)
    g_ref[...] = jax.nn.sigmoid(proj(_OFF_G, 2 * D_MODEL) + bg_ref[...])
    dt_ref[...] = _softplus(proj(_OFF_DT, DT_PAD) + dtb_row_ref[...])
    dtt_ref[...] = _softplus(proj_t(_ROW_DT, DT_PAD) + dtb_col_ref[...])
    vt_ref[...] = proj_t(_ROW_V, KV_WIDTH).astype(BF16)

    half = HEAD_DIM // 2
    cos = cos_ref[...]
    sin = sin_ref[...]
    for h in range(N_KV_HEADS):
        x = _rms(proj(_OFF_K + h * HEAD_DIM, HEAD_DIM), kg_ref[...])
        k_ref[:, h * HEAD_DIM:(h + 1) * HEAD_DIM] = (x * cos + pltpu.roll(x, half, axis=1) * sin).astype(BF16)

    cos_t = cos_t_ref[...]
    sin_t = sin_t_ref[...]
    q_scale = HEAD_DIM ** -0.5 * LOG2_E
    for h in range(N_Q_HEADS):
        x = proj_t(_ROW_Q + h * HEAD_DIM, HEAD_DIM)
        x = x * lax.rsqrt(jnp.mean(x * x, axis=0, keepdims=True) + EPS) * qg_ref[...]
        rot = jnp.concatenate([x[half:], x[:half]], axis=0)
        qt_ref[h * HEAD_DIM:(h + 1) * HEAD_DIM, :] = ((x * cos_t + rot * sin_t) * q_scale).astype(BF16)


def _inproj(u, P, tables, *, tm=256):
    b, L, _ = u.shape

    def tok(width):
        return pl.BlockSpec((None, tm, width), lambda bi, i: (bi, i, 0))

    def tok_t(rows):
        return pl.BlockSpec((None, rows, tm), lambda bi, i: (bi, 0, i))

    pos = pl.BlockSpec((tm, HEAD_DIM), lambda bi, i: (i, 0))
    pos_t = pl.BlockSpec((HEAD_DIM, tm), lambda bi, i: (0, i))
    outs = [(tok, D_INNER, F32), (tok, CONV_DIM, F32), (tok_t, ATTN_WIDTH, BF16), (tok, KV_WIDTH, BF16),
            (tok_t, KV_WIDTH, BF16), (tok, 2 * D_MODEL, F32), (tok, DT_PAD, F32), (tok_t, DT_PAD, F32)]
    out_shape = [jax.ShapeDtypeStruct((b, L, w) if mk is tok else (b, w, L), dt) for mk, w, dt in outs]
    out_specs = [mk(w) for mk, w, _ in outs]
    cos, sin, cos_t, sin_t = tables
    return pl.pallas_call(
        _inproj_kernel,
        grid=(b, L // tm),
        in_specs=[tok(D_MODEL), _resident((D_MODEL, _IN_COLS)), _resident((_T_ROWS, D_MODEL)), pos, pos,
                  pos_t, pos_t, _resident((HEAD_DIM, 1)), _resident((1, HEAD_DIM)), _resident((1, 2 * D_MODEL)),
                  _resident((1, DT_PAD)), _resident((DT_PAD, 1))],
        out_specs=out_specs,
        out_shape=out_shape,
        compiler_params=_params("parallel", "parallel"),
        name="in_proj",
    )(u, P["w_in"], P["w_in_t"], cos, sin, cos_t, sin_t, P["q_gain_col"], P["k_gain"], P["b_gate"],
      P["dt_bias_row"], P["dt_bias_col"])


def _conv_kernel(x_ref, prev_ref, next_ref, w_ref, b_ref, o_ref, ext_ref, *, tl):
    i = pl.program_id(1)
    n = pl.num_programs(1)
    halo = SUBLANES
    ext_ref[0:halo, :] = jnp.where(i > 0, prev_ref[...], 0.0)
    ext_ref[halo:halo + tl, :] = x_ref[...]
    ext_ref[halo + tl:halo + tl + halo, :] = jnp.where(i < n - 1, next_ref[...], 0.0)
    pad = D_CONV // 2
    acc = jnp.zeros((tl, CONV_DIM), F32) + b_ref[...]
    for j in range(D_CONV):
        acc = acc + ext_ref[halo - pad + j:halo - pad + j + tl, :] * w_ref[j:j + 1, :]
    o_ref[...] = _silu(acc)


def _conv(xbc, conv_w, conv_b, *, tl=256):
    b, L, _ = xbc.shape
    r = tl // SUBLANES
    last = L // SUBLANES - 1
    return pl.pallas_call(
        functools.partial(_conv_kernel, tl=tl),
        grid=(b, L // tl),
        in_specs=[pl.BlockSpec((None, tl, CONV_DIM), lambda bi, i: (bi, i, 0)),
                  pl.BlockSpec((None, SUBLANES, CONV_DIM), lambda bi, i: (bi, jnp.maximum(i * r - 1, 0), 0)),
                  pl.BlockSpec((None, SUBLANES, CONV_DIM), lambda bi, i: (bi, jnp.minimum((i + 1) * r, last), 0)),
                  _resident((SUBLANES, CONV_DIM)), _resident((1, CONV_DIM))],
        out_specs=pl.BlockSpec((None, tl, CONV_DIM), lambda bi, i: (bi, i, 0)),
        out_shape=jax.ShapeDtypeStruct((b, L, CONV_DIM), F32),
        scratch_shapes=[pltpu.VMEM((tl + 2 * SUBLANES, CONV_DIM), F32)],
        compiler_params=_params("parallel", "parallel"),
        name="conv",
    )(xbc, xbc, xbc, conv_w, conv_b)


DUP = 2 * SSM_HEADS


def _chunk_masks(rev):
    row_i = lax.broadcasted_iota(jnp.int32, (CHUNK, CHUNK), 0)
    col_i = lax.broadcasted_iota(jnp.int32, (CHUNK, CHUNK), 1)
    tri = (col_i >= row_i) if rev else (col_i <= row_i)
    return tri, tri.astype(F32).astype(BF16)


def _split3(x):
    hi = x.astype(BF16)
    r = x - hi.astype(F32)
    mid = r.astype(BF16)
    lo = (r - mid.astype(F32)).astype(BF16)
    return hi, mid, lo


def _ssd_prepare(dt2, dt_t, a2_row, a_col, bm, cm, expand_ref, tri_b, rev):
    tri3 = jnp.concatenate([tri_b, tri_b, tri_b], axis=1)
    acum2 = jnp.dot(tri3, jnp.concatenate(_split3(dt2 * a2_row), axis=0),
                    preferred_element_type=F32)
    acum_t = lax.dot_general(jnp.concatenate(_split3(dt_t * a_col), axis=1), tri3, (((1,), (1,)), ((), ())),
                             preferred_element_type=F32)
    end = 0 if rev else CHUNK - 1
    total2 = acum2[end:end + 1, :]
    ew = jnp.concatenate([jnp.exp2(acum2), jnp.exp2(total2 - acum2) * dt2], axis=0)
    ew_hi = ew.astype(BF16)
    ew_lo = (ew - ew_hi.astype(F32)).astype(BF16)
    lane = lax.broadcasted_iota(jnp.int32, ew.shape, 1)
    rep = jnp.dot(jnp.where(lane < SSM_HEADS, ew_hi, ew_lo), expand_ref[...],
                  preferred_element_type=F32)
    r_t = acum_t - jnp.log2(dt_t)
    b_gs, c_gs, cbs = [], [], []
    for g in range(SSM_GROUPS):
        b_g = bm[:, g * D_STATE:(g + 1) * D_STATE].astype(BF16)
        c_g = cm[:, g * D_STATE:(g + 1) * D_STATE].astype(BF16)
        b_gs.append(b_g)
        c_gs.append(c_g)
        cbs.append(lax.dot_general(c_g, b_g, (((1,), (1,)), ((), ())), preferred_element_type=F32))
    return acum2, r_t, rep[:CHUNK], rep[CHUNK:], b_gs, c_gs, cbs


def _ssd_finish(prep, xs, state_ref, tri, rev):
    acum2, r_t, e_rep, w_rep, b_gs, c_gs, cbs = prep
    end = 0 if rev else CHUNK - 1
    cd_rep = e_rep[end:end + 1, :]
    y_offs = []
    for g in range(SSM_GROUPS):
        cols = slice(g * GROUP_WIDTH, (g + 1) * GROUP_WIDTH)
        prev = state_ref[g]
        y_offs.append(jnp.dot(c_gs[g], prev.astype(BF16), preferred_element_type=F32) * e_rep[:, cols])
        s_new = lax.dot_general(b_gs[g], (xs[:, cols] * w_rep[:, cols]).astype(BF16), (((0,), (0,)), ((), ())),
                                preferred_element_type=F32)
        state_ref[g] = prev * cd_rep[:, cols] + s_new
    lane = lax.broadcasted_iota(jnp.int32, (CHUNK, LANES), 1)
    lo_half = lane < SSM_HEAD_DIM
    ys = []
    for g in range(SSM_GROUPS):
        pairs = []
        for j in range(GROUP_WIDTH // LANES):
            ms = []
            for hh in range(2):
                h = g * (SSM_HEADS // SSM_GROUPS) + 2 * j + hh
                seg = acum2[:, h:h + 1] - r_t[h:h + 1, :]
                ms.append(cbs[g] * jnp.exp2(jnp.where(tri, seg, -jnp.inf)))
            lhs = jnp.concatenate(ms, axis=1).astype(BF16)
            xp = xs[:, g * GROUP_WIDTH + j * LANES:g * GROUP_WIDTH + (j + 1) * LANES]
            rhs = jnp.concatenate([jnp.where(lo_half, xp, 0.0), jnp.where(lo_half, 0.0, xp)],
                                  axis=0).astype(BF16)
            pairs.append(jnp.dot(lhs, rhs, preferred_element_type=F32))
        ys.append(jnp.concatenate(pairs, axis=1) + y_offs[g])
    return jnp.concatenate(ys, axis=1)


def _ssd_scan(xc_ref, dt_ref, dtt_ref, alog_row_ref, alog_col_ref, expand_ref, state_ref, emit, *, nchunk, rev):
    @pl.when(pl.program_id(1) == 0)
    def _():
        state_ref[...] = jnp.zeros_like(state_ref)

    lo, lo_t = (DUP, SSM_HEADS) if rev else (0, 0)
    a2_row = -jnp.exp(alog_row_ref[:, lo:lo + DUP]) * LOG2_E
    a_col = -jnp.exp(alog_col_ref[lo_t:lo_t + SSM_HEADS, :]) * LOG2_E
    tri, tri_b = _chunk_masks(rev)

    def prepare(c):
        rows = slice(c * CHUNK, (c + 1) * CHUNK)
        return _ssd_prepare(dt_ref[rows, lo:lo + DUP], dtt_ref[lo_t:lo_t + SSM_HEADS, rows], a2_row, a_col,
                            xc_ref[rows, D_INNER:D_INNER + BC_WIDTH], xc_ref[rows, D_INNER + BC_WIDTH:],
                            expand_ref, tri_b, rev)

    order = list(range(nchunk))[::-1] if rev else list(range(nchunk))
    nxt = prepare(order[0])
    for i, c in enumerate(order):
        cur = nxt
        if i + 1 < nchunk:
            nxt = prepare(order[i + 1])
        rows = slice(c * CHUNK, (c + 1) * CHUNK)
        xs = xc_ref[rows, :D_INNER]
        emit(rows, xs, _ssd_finish(cur, xs, state_ref, tri, rev))


def _ssd_fwd_kernel(xc_ref, dt_ref, dtt_ref, alog_row_ref, alog_col_ref, expand_ref, y_ref, state_ref, *, nchunk):
    def emit(rows, xs, y):
        y_ref[rows, :] = y

    _ssd_scan(xc_ref, dt_ref, dtt_ref, alog_row_ref, alog_col_ref, expand_ref, state_ref, emit,
              nchunk=nchunk, rev=False)


def _ssd_bwd_kernel(xc_ref, dt_ref, dtt_ref, alog_row_ref, alog_col_ref, expand_ref, yf_ref, z_ref, dskip_ref,
                    gain_ref, s_ref, state_ref, *, nchunk):
    def emit(rows, xs, y_b):
        y = (yf_ref[rows, :] + y_b + dskip_ref[...] * xs) * _silu(z_ref[rows, :])
        for g in range(SSM_GROUPS):
            cols = slice(g * GROUP_WIDTH, (g + 1) * GROUP_WIDTH)
            s_ref[rows, cols] = _rms(y[:, cols], gain_ref[:, cols]).astype(s_ref.dtype)

    _ssd_scan(xc_ref, dt_ref, dtt_ref, alog_row_ref, alog_col_ref, expand_ref, state_ref, emit,
              nchunk=nchunk, rev=True)


def _ssd(xc, dt, dt_t, z, P, *, nchunk=2):
    b, L, _ = xc.shape
    tl = nchunk * CHUNK
    nb = L // tl

    def tok(width, rev):
        if rev:
            return pl.BlockSpec((None, tl, width), lambda bi, c: (bi, nb - 1 - c, 0))
        return pl.BlockSpec((None, tl, width), lambda bi, c: (bi, c, 0))

    def tok_t(rev):
        if rev:
            return pl.BlockSpec((None, DT_PAD, tl), lambda bi, c: (bi, 0, nb - 1 - c))
        return pl.BlockSpec((None, DT_PAD, tl), lambda bi, c: (bi, 0, c))

    state = pltpu.VMEM((SSM_GROUPS, D_STATE, GROUP_WIDTH), F32)
    consts = [_resident((1, DT_PAD)), _resident((DT_PAD, 1)), _resident((DUP, D_INNER))]
    y_f = pl.pallas_call(
        functools.partial(_ssd_fwd_kernel, nchunk=nchunk),
        grid=(b, nb),
        in_specs=[tok(CONV_DIM, False), tok(DT_PAD, False), tok_t(False)] + consts,
        out_specs=tok(D_INNER, False),
        out_shape=jax.ShapeDtypeStruct((b, L, D_INNER), F32),
        scratch_shapes=[state],
        compiler_params=_params("parallel", "arbitrary"),
        name="ssd_fwd",
    )(xc, dt, dt_t, P["a_log_row"], P["a_log_col"], P["head_expand"])
    return pl.pallas_call(
        functools.partial(_ssd_bwd_kernel, nchunk=nchunk),
        grid=(b, nb),
        in_specs=[tok(CONV_DIM, True), tok(DT_PAD, True), tok_t(True)] + consts
                 + [tok(D_INNER, True), tok(D_INNER, True), _resident((1, D_INNER)), _resident((1, D_INNER))],
        out_specs=tok(D_INNER, True),
        out_shape=jax.ShapeDtypeStruct((b, L, D_INNER), BF16),
        scratch_shapes=[state],
        compiler_params=_params("parallel", "arbitrary"),
        name="ssd_bwd",
    )(xc, dt, dt_t, P["a_log_row"], P["a_log_col"], P["head_expand"], y_f, z, P["d_skip_rep"], P["ssm_norm"])


ONES_ROWS = 16
QK_LOOKAHEAD = 3


def _attn_kernel(qt_ref, k_ref, vt_ref, o_ref, m_ref, acc_ref, *, tq, nb):
    kv = pl.program_id(3)

    @pl.when(kv == 0)
    def _():
        m_ref[...] = jnp.full_like(m_ref, -jnp.inf)
        acc_ref[...] = jnp.zeros_like(acc_ref)

    k = k_ref[...]
    vt = vt_ref[...]
    vt_aug = jnp.concatenate([vt, jnp.ones((ONES_ROWS, vt.shape[1]), BF16)], axis=0)
    blocks = [(g, c) for g in range(Q_PER_KV) for c in range(tq // nb)]

    def scores(g, c):
        qt = qt_ref[g * HEAD_DIM:(g + 1) * HEAD_DIM, c * nb:(c + 1) * nb]
        return jnp.dot(k, qt, preferred_element_type=F32)

    pending = [scores(*blk) for blk in blocks[:QK_LOOKAHEAD]]
    for i, (g, c) in enumerate(blocks):
        s = pending.pop(0)
        if i + QK_LOOKAHEAD < len(blocks):
            pending.append(scores(*blocks[i + QK_LOOKAHEAD]))
        cols = slice(g * tq + c * nb, g * tq + (c + 1) * nb)
        m_prev = m_ref[:, cols]
        m_new = jnp.maximum(m_prev, jnp.max(s, axis=0, keepdims=True))
        alpha = jnp.exp2(m_prev - m_new)
        p = jnp.exp2(s - m_new).astype(BF16)
        acc_ref[:, cols] = alpha * acc_ref[:, cols] + jnp.dot(vt_aug, p, preferred_element_type=F32)
        m_ref[:, cols] = m_new

    @pl.when(kv == pl.num_programs(3) - 1)
    def _():
        for g in range(Q_PER_KV):
            acc = acc_ref[:, g * tq:(g + 1) * tq]
            out_t = acc[:HEAD_DIM] / acc[HEAD_DIM:HEAD_DIM + 1]
            o_ref[:, g * HEAD_DIM:(g + 1) * HEAD_DIM] = out_t.T.astype(o_ref.dtype)


def _attn_bounded_kernel(qt_ref, k_ref, vt_ref, o_ref, acc_ref, *, tq, nb, kb):
    kv = pl.program_id(3)

    @pl.when(kv == 0)
    def _():
        acc_ref[...] = jnp.zeros_like(acc_ref)

    tk = k_ref.shape[0]
    ones = jnp.ones((ONES_ROWS, kb), BF16)
    vt_aug = [jnp.concatenate([vt_ref[:, t * kb:(t + 1) * kb], ones], axis=0) for t in range(tk // kb)]
    blocks = [(t, g, c) for t in range(tk // kb) for g in range(Q_PER_KV) for c in range(tq // nb)]

    def scores(t, g, c):
        qt = qt_ref[g * HEAD_DIM:(g + 1) * HEAD_DIM, c * nb:(c + 1) * nb]
        return jnp.dot(k_ref[t * kb:(t + 1) * kb, :], qt, preferred_element_type=F32)

    pending = [scores(*blk) for blk in blocks[:QK_LOOKAHEAD]]
    for i, (t, g, c) in enumerate(blocks):
        s = pending.pop(0)
        if i + QK_LOOKAHEAD < len(blocks):
            pending.append(scores(*blocks[i + QK_LOOKAHEAD]))
        cols = slice(g * tq + c * nb, g * tq + (c + 1) * nb)
        acc_ref[:, cols] += jnp.dot(vt_aug[t], jnp.exp2(s).astype(BF16), preferred_element_type=F32)

    @pl.when(kv == pl.num_programs(3) - 1)
    def _():
        for g in range(Q_PER_KV):
            acc = acc_ref[:, g * tq:(g + 1) * tq]
            out_t = acc[:HEAD_DIM] / acc[HEAD_DIM:HEAD_DIM + 1]
            o_ref[:, g * HEAD_DIM:(g + 1) * HEAD_DIM] = out_t.T.astype(o_ref.dtype)


def _attention(qt, k, vt, *, bounded, tq, tk, nb=256):
    b, L, _ = k.shape
    gw = Q_PER_KV * HEAD_DIM
    acc = pltpu.VMEM((HEAD_DIM + ONES_ROWS, Q_PER_KV * tq), F32)
    if bounded:
        body, scratch = functools.partial(_attn_bounded_kernel, kb=min(tk, 512)), [acc]
    else:
        body, scratch = _attn_kernel, [pltpu.VMEM((1, Q_PER_KV * tq), F32), acc]
    return pl.pallas_call(
        functools.partial(body, tq=tq, nb=nb),
        grid=(b, N_KV_HEADS, L // tq, L // tk),
        in_specs=[pl.BlockSpec((None, gw, tq), lambda bi, h, i, j: (bi, h, i)),
                  pl.BlockSpec((None, tk, HEAD_DIM), lambda bi, h, i, j: (bi, j, h)),
                  pl.BlockSpec((None, HEAD_DIM, tk), lambda bi, h, i, j: (bi, h, j))],
        out_specs=pl.BlockSpec((None, tq, gw), lambda bi, h, i, j: (bi, i, h)),
        out_shape=jax.ShapeDtypeStruct((b, L, ATTN_WIDTH), BF16),
        scratch_shapes=scratch,
        compiler_params=_params("parallel", "parallel", "parallel", "arbitrary"),
        name="attn_bounded" if bounded else "attn",
    )(qt, k, vt)


SCORE_BOUND = 40.0
_BF16_NORM_SLACK = (1.0 + 2.0 ** -8) ** 2


def _attention_dispatch(qt, k, vt, q_gain, k_gain):
    L = k.shape[1]
    bound = (LOG2_E * HEAD_DIM ** 0.5 * _BF16_NORM_SLACK) * jnp.max(jnp.abs(q_gain)) * jnp.max(jnp.abs(k_gain))
    t_fast = 1024 if L % 1024 == 0 else 512
    return lax.cond(bound <= SCORE_BOUND,
                    functools.partial(_attention, bounded=True, tq=t_fast, tk=t_fast),
                    functools.partial(_attention, bounded=False, tq=512, tk=512),
                    qt, k, vt)


def _merge_kernel(s_ref, a_ref, g_ref, h_ref, ws_ref, wa_ref, wo_ref, o_ref):
    ps = jnp.dot(s_ref[...], ws_ref[...], preferred_element_type=F32)
    pa = jnp.dot(a_ref[...], wa_ref[...], preferred_element_type=F32)
    m = g_ref[:, :D_MODEL] * ps + g_ref[:, D_MODEL:] * pa
    o_ref[...] = h_ref[...] + jnp.dot(m.astype(BF16), wo_ref[...], preferred_element_type=F32)


def _merge(s, a, g, h, P, *, tm=512):
    b, L, _ = h.shape

    def tok(width):
        return pl.BlockSpec((None, tm, width), lambda bi, i: (bi, i, 0))

    return pl.pallas_call(
        _merge_kernel,
        grid=(b, L // tm),
        in_specs=[tok(D_INNER), tok(ATTN_WIDTH), tok(2 * D_MODEL), tok(D_MODEL),
                  _resident((D_INNER, D_MODEL)), _resident((ATTN_WIDTH, D_MODEL)),
                  _resident((D_MODEL, D_MODEL))],
        out_specs=tok(D_MODEL),
        out_shape=jax.ShapeDtypeStruct((b, L, D_MODEL), F32),
        compiler_params=_params("parallel", "parallel"),
        name="merge",
    )(s, a, g, h, P["w_ssm_branch"], P["w_attn_branch"], P["w_out"])


def _rope_tables(L):
    rows = L // GRID_W
    row = jnp.repeat(jnp.arange(rows, dtype=F32), GRID_W)
    col = jnp.tile(jnp.arange(GRID_W, dtype=F32), rows)
    axis_dim = HEAD_DIM // 2
    inv_freq = ROPE_THETA ** (-jnp.arange(0, axis_dim, 2, dtype=F32) / axis_dim)
    ang = jnp.concatenate([row[:, None] * inv_freq, col[:, None] * inv_freq], axis=-1)
    cos, sin = jnp.cos(ang), jnp.sin(ang)
    cos2, sin2 = jnp.concatenate([cos, cos], axis=-1), jnp.concatenate([-sin, sin], axis=-1)
    return cos2, sin2, cos2.T, sin2.T


def _layer_params(norm_ffn1, w_ffn1_gu, w_ffn1_down, norm_mix, w_in, conv_w, conv_b, dt_bias_f, dt_bias_b,
                  A_log_f, A_log_b, D_skip, ssm_norm, q_norm, k_norm, w_ssm_branch, w_attn_branch, b_gate,
                  w_out, norm_ffn2, w_ffn2_gu, w_ffn2_down):
    sizes = (D_INNER, CONV_DIM, SSM_HEADS, SSM_HEADS, ATTN_WIDTH, KV_WIDTH, KV_WIDTH, 2 * D_MODEL)
    splits = [int(v) for v in np.cumsum(sizes)[:-1]]
    w_z, w_xbc, w_dtf, w_dtb, w_q, w_k, w_v, w_g = jnp.split(w_in, splits, axis=-1)
    perm = np.concatenate([np.arange(0, HEAD_DIM, 2), np.arange(1, HEAD_DIM, 2)])
    q_cols = np.concatenate([h * HEAD_DIM + perm for h in range(N_Q_HEADS)])
    k_cols = np.concatenate([h * HEAD_DIM + perm for h in range(N_KV_HEADS)])
    w_dt_dup = jnp.concatenate([w_dtf, w_dtf, w_dtb, w_dtb], axis=-1)
    w_dt_pad = jnp.concatenate([w_dtf, w_dtb, jnp.zeros((D_MODEL, DT_PAD - 2 * SSM_HEADS), F32)], axis=-1)
    w_r = jnp.concatenate([w_z, w_xbc, w_k[:, k_cols], w_g, w_dt_dup], axis=-1)
    w_t = jnp.concatenate([w_q[:, q_cols], w_v, w_dt_pad], axis=-1).T
    pad = jnp.zeros((DT_PAD - 2 * SSM_HEADS,), F32)
    head_expand = jnp.repeat(jnp.tile(jnp.eye(SSM_HEADS, dtype=BF16), (2, 1)), SSM_HEAD_DIM, axis=1)
    row = lambda v: v.reshape(1, -1).astype(F32)
    return dict(
        norm_ffn1=row(norm_ffn1), w_ffn1_gu=w_ffn1_gu.astype(BF16), w_ffn1_down=w_ffn1_down.astype(BF16),
        norm_mix=row(norm_mix), w_in=w_r.astype(BF16), w_in_t=w_t.astype(BF16),
        conv_w=jnp.concatenate([conv_w, jnp.zeros((SUBLANES - D_CONV, CONV_DIM), F32)], axis=0),
        conv_b=row(conv_b),
        dt_bias_row=row(jnp.concatenate([dt_bias_f, dt_bias_f, dt_bias_b, dt_bias_b])),
        dt_bias_col=jnp.concatenate([dt_bias_f, dt_bias_b, pad]).reshape(-1, 1),
        a_log_row=row(jnp.concatenate([A_log_f, A_log_f, A_log_b, A_log_b])),
        a_log_col=jnp.concatenate([A_log_f, A_log_b, pad]).reshape(-1, 1), head_expand=head_expand,
        d_skip_rep=row(jnp.repeat(D_skip, SSM_HEAD_DIM)), ssm_norm=row(ssm_norm),
        q_gain_col=q_norm[perm].reshape(-1, 1).astype(F32), k_gain=row(k_norm[perm]),
        w_ssm_branch=w_ssm_branch.astype(BF16), w_attn_branch=w_attn_branch.astype(BF16),
        b_gate=row(b_gate), w_out=w_out.astype(BF16),
        norm_ffn2=row(norm_ffn2), w_ffn2_gu=w_ffn2_gu.astype(BF16), w_ffn2_down=w_ffn2_down.astype(BF16),
    )


def _encoder_layer(x, P, g_next, last):
    L = x.shape[1]
    h, u = _ffn(x, P["norm_ffn1"], P["w_ffn1_gu"], P["w_ffn1_down"], P["norm_mix"],
                emit_h=True, post_dtype=BF16)
    z, xbc, q_t, k, v_t, g, dt, dt_t = _inproj(u, P, _rope_tables(L))
    xc = _conv(xbc, P["conv_w"], P["conv_b"])
    s = _ssd(xc, dt, dt_t, z, P)
    a = _attention_dispatch(q_t, k, v_t, P["q_gain_col"], P["k_gain"])
    h2 = _merge(s, a, g, h, P)
    outs = _ffn(h2, P["norm_ffn2"], P["w_ffn2_gu"], P["w_ffn2_down"], g_next,
                emit_h=not last, post_dtype=F32)
    return outs


def _trunk(x, layers, norm_final):
    depth = len(layers)
    for i, P in enumerate(layers):
        last = i == depth - 1
        g_next = norm_final.reshape(1, -1) if last else layers[i + 1]["norm_ffn1"]
        outs = _encoder_layer(x, P, g_next, last)
        x = outs[0]
    return x


def kernel(x_prompt, x_sample, norm_ffn1, w_ffn1_gu, w_ffn1_down, norm_mix, w_in, conv_w, conv_b, dt_bias_f,
           dt_bias_b, A_log_f, A_log_b, D_skip, ssm_norm, q_norm, k_norm, w_ssm_branch, w_attn_branch, b_gate,
           w_out, norm_ffn2, w_ffn2_gu, w_ffn2_down, norm_final):
    per_layer = (norm_ffn1, w_ffn1_gu, w_ffn1_down, norm_mix, w_in, conv_w, conv_b, dt_bias_f, dt_bias_b,
                 A_log_f, A_log_b, D_skip, ssm_norm, q_norm, k_norm, w_ssm_branch, w_attn_branch, b_gate,
                 w_out, norm_ffn2, w_ffn2_gu, w_ffn2_down)
    depth = norm_ffn1.shape[0]
    layers = [_layer_params(*(w[i] for w in per_layer)) for i in range(depth)]
    return (_trunk(x_prompt, layers, norm_final), _trunk(x_sample, layers, norm_final))
```

```python
import functools

import jax
import jax.numpy as jnp
import numpy as np
from jax import lax
from jax.experimental import pallas as pl
from jax.experimental.pallas import tpu as pltpu

F32 = jnp.float32
BF16 = jnp.bfloat16

D_MODEL = 1024
GRID_W = 64
D_INNER = 2048
SSM_HEAD_DIM = 64
SSM_HEADS = 32
SSM_GROUPS = 4
GROUP_WIDTH = D_INNER // SSM_GROUPS
D_STATE = 128
D_CONV = 5
CHUNK = 128
BC_WIDTH = SSM_GROUPS * D_STATE
CONV_DIM = D_INNER + 2 * BC_WIDTH
HEAD_DIM = 128
N_Q_HEADS = 8
N_KV_HEADS = 2
Q_PER_KV = N_Q_HEADS // N_KV_HEADS
ATTN_WIDTH = N_Q_HEADS * HEAD_DIM
KV_WIDTH = N_KV_HEADS * HEAD_DIM
ROPE_THETA = 10000.0
D_FF = 2816
EPS = 1e-6
DT_PAD = 128

LANES = 128
SUBLANES = 8
VMEM_LIMIT = 56 * 1024 * 1024

LOG2_E = 1.4426950408889634

_OFF_Z = 0
_OFF_XBC = _OFF_Z + D_INNER
_OFF_K = _OFF_XBC + CONV_DIM
_OFF_G = _OFF_K + KV_WIDTH
_OFF_DT = _OFF_G + 2 * D_MODEL
_IN_COLS = _OFF_DT + DT_PAD
_ROW_Q = 0
_ROW_V = _ROW_Q + ATTN_WIDTH
_ROW_DT = _ROW_V + KV_WIDTH
_T_ROWS = _ROW_DT + DT_PAD


def _rms(x, g):
    return x * lax.rsqrt(jnp.mean(x * x, axis=-1, keepdims=True) + EPS) * g


def _silu(x):
    return x * jax.nn.sigmoid(x)


def _softplus(x):
    return jnp.maximum(x, 0.0) + jnp.log1p(jnp.exp(-jnp.abs(x)))


def _params(*sem):
    return pltpu.CompilerParams(dimension_semantics=sem, vmem_limit_bytes=VMEM_LIMIT)


def _resident(shape):
    nd = len(shape)
    return pl.BlockSpec(shape, lambda *_: (0,) * nd, pipeline_mode=pl.Buffered(1))


def _ffn_kernel(x_ref, gpre_ref, wgu_ref, wd_ref, gpost_ref, *out_refs, tf, emit_h):
    x = x_ref[...]
    xn = _rms(x, gpre_ref[...]).astype(BF16)
    acc = jnp.zeros_like(x)
    for c in range(D_FF // tf):
        g = jnp.dot(xn, wgu_ref[:, c * tf:(c + 1) * tf], preferred_element_type=F32)
        u = jnp.dot(xn, wgu_ref[:, D_FF + c * tf:D_FF + (c + 1) * tf], preferred_element_type=F32)
        act = (_silu(g) * u).astype(BF16)
        acc = acc + jnp.dot(act, wd_ref[c * tf:(c + 1) * tf, :], preferred_element_type=F32)
    h = x + 0.5 * acc
    if emit_h:
        out_refs[0][...] = h
    post_ref = out_refs[-1]
    post_ref[...] = _rms(h, gpost_ref[...]).astype(post_ref.dtype)


def _ffn(x, g_pre, w_gu, w_down, g_post, *, emit_h, post_dtype, tm=512, tf=256):
    b, L, _ = x.shape
    tok = pl.BlockSpec((None, tm, D_MODEL), lambda bi, i: (bi, i, 0))
    out_shape = [jax.ShapeDtypeStruct((b, L, D_MODEL), post_dtype)]
    out_specs = [tok]
    if emit_h:
        out_shape.insert(0, jax.ShapeDtypeStruct((b, L, D_MODEL), F32))
        out_specs.insert(0, tok)
    return pl.pallas_call(
        functools.partial(_ffn_kernel, tf=tf, emit_h=emit_h),
        grid=(b, L // tm),
        in_specs=[tok, _resident((1, D_MODEL)), _resident((D_MODEL, 2 * D_FF)),
                  _resident((D_FF, D_MODEL)), _resident((1, D_MODEL))],
        out_specs=out_specs,
        out_shape=out_shape,
        compiler_params=_params("parallel", "parallel"),
        name="ffn",
    )(x, g_pre, w_gu, w_down, g_post)


CONV_CHUNK = 512
U_HALO = 16


def _inproj_kernel(u_ref, uprev_ref, unext_ref, w_ref, wt_ref, cos_ref, sin_ref, cos_t_ref, sin_t_ref, qg_ref,
                   kg_ref, bg_ref, dtb_row_ref, dtb_col_ref, cw_ref, cb_ref,
                   z_ref, xc_ref, qt_ref, k_ref, vt_ref, g_ref, dt_ref, dtt_ref, ext_ref):
    u = u_ref[...]
    tm = u.shape[0]

    def proj(off, width):
        return jnp.dot(u, w_ref[:, off:off + width], preferred_element_type=F32)

    def proj_t(off, rows):
        return lax.dot_general(wt_ref[off:off + rows, :], u, (((1,), (1,)), ((), ())),
                               preferred_element_type=F32)

    i = pl.program_id(1)
    u_ext = jnp.concatenate([uprev_ref[...], u, unext_ref[...]], axis=0)
    first = U_HALO - D_CONV // 2

    def project_chunk(c):
        cols = slice(c * CONV_CHUNK, (c + 1) * CONV_CHUNK)
        x = jnp.dot(u_ext, w_ref[:, _OFF_XBC + c * CONV_CHUNK:_OFF_XBC + (c + 1) * CONV_CHUNK],
                    preferred_element_type=F32)
        ext_ref[0:U_HALO, cols] = jnp.where(i > 0, x[0:U_HALO], 0.0)
        ext_ref[U_HALO:U_HALO + tm, cols] = x[U_HALO:U_HALO + tm]
        ext_ref[U_HALO + tm:, cols] = jnp.where(i < pl.num_programs(1) - 1, x[U_HALO + tm:], 0.0)

    def conv_chunk(c):
        cols = slice(c * CONV_CHUNK, (c + 1) * CONV_CHUNK)
        acc = jnp.zeros((tm, CONV_CHUNK), F32) + cb_ref[:, cols]
        for j in range(D_CONV):
            acc = acc + ext_ref[first + j:first + j + tm, cols] * cw_ref[j:j + 1, cols]
        xc_ref[:, cols] = _silu(acc)

    n_chunks = CONV_DIM // CONV_CHUNK
    project_chunk(0)
    for c in range(n_chunks):
        if c + 1 < n_chunks:
            project_chunk(c + 1)
        conv_chunk(c)
    z_ref[...] = proj(_OFF_Z, D_INNER)
    g_ref[...] = jax.nn.sigmoid(proj(_OFF_G, 2 * D_MODEL) + bg_ref[...])
    dt_ref[...] = _softplus(proj(_OFF_DT, DT_PAD) + dtb_row_ref[...])
    dtt_ref[...] = _softplus(proj_t(_ROW_DT, DT_PAD) + dtb_col_ref[...])
    vt_ref[...] = proj_t(_ROW_V, KV_WIDTH).astype(BF16)

    half = HEAD_DIM // 2
    cos = cos_ref[...]
    sin = sin_ref[...]
    for h in range(N_KV_HEADS):
        x = _rms(proj(_OFF_K + h * HEAD_DIM, HEAD_DIM), kg_ref[...])
        k_ref[:, h * HEAD_DIM:(h + 1) * HEAD_DIM] = (x * cos + pltpu.roll(x, half, axis=1) * sin).astype(BF16)

    cos_t = cos_t_ref[...]
    sin_t = sin_t_ref[...]
    q_scale = HEAD_DIM ** -0.5 * LOG2_E
    for h in range(N_Q_HEADS):
        x = proj_t(_ROW_Q + h * HEAD_DIM, HEAD_DIM)
        x = x * lax.rsqrt(jnp.mean(x * x, axis=0, keepdims=True) + EPS) * qg_ref[...]
        rot = jnp.concatenate([x[half:], x[:half]], axis=0)
        qt_ref[h * HEAD_DIM:(h + 1) * HEAD_DIM, :] = ((x * cos_t + rot * sin_t) * q_scale).astype(BF16)


def _inproj(u, P, tables, *, tm=256):
    b, L, _ = u.shape

    def tok(width):
        return pl.BlockSpec((None, tm, width), lambda bi, i: (bi, i, 0))

    def tok_t(rows):
        return pl.BlockSpec((None, rows, tm), lambda bi, i: (bi, 0, i))

    pos = pl.BlockSpec((tm, HEAD_DIM), lambda bi, i: (i, 0))
    pos_t = pl.BlockSpec((HEAD_DIM, tm), lambda bi, i: (0, i))
    outs = [(tok, D_INNER, F32), (tok, CONV_DIM, F32), (tok_t, ATTN_WIDTH, BF16), (tok, KV_WIDTH, BF16),
            (tok_t, KV_WIDTH, BF16), (tok, 2 * D_MODEL, F32), (tok, DT_PAD, F32), (tok_t, DT_PAD, F32)]
    out_shape = [jax.ShapeDtypeStruct((b, L, w) if mk is tok else (b, w, L), dt) for mk, w, dt in outs]
    out_specs = [mk(w) for mk, w, _ in outs]
    cos, sin, cos_t, sin_t = tables
    r = tm // U_HALO
    last = L // U_HALO - 1
    halo_prev = pl.BlockSpec((None, U_HALO, D_MODEL), lambda bi, i: (bi, jnp.maximum(i * r - 1, 0), 0))
    halo_next = pl.BlockSpec((None, U_HALO, D_MODEL), lambda bi, i: (bi, jnp.minimum((i + 1) * r, last), 0))
    return pl.pallas_call(
        _inproj_kernel,
        grid=(b, L // tm),
        in_specs=[tok(D_MODEL), halo_prev, halo_next, _resident((D_MODEL, _IN_COLS)),
                  _resident((_T_ROWS, D_MODEL)), pos, pos,
                  pos_t, pos_t, _resident((HEAD_DIM, 1)), _resident((1, HEAD_DIM)), _resident((1, 2 * D_MODEL)),
                  _resident((1, DT_PAD)), _resident((DT_PAD, 1)),
                  _resident((SUBLANES, CONV_DIM)), _resident((1, CONV_DIM))],
        out_specs=out_specs,
        out_shape=out_shape,
        scratch_shapes=[pltpu.VMEM((tm + 2 * U_HALO, CONV_DIM), F32)],
        compiler_params=_params("parallel", "parallel"),
        name="in_proj",
    )(u, u, u, P["w_in"], P["w_in_t"], cos, sin, cos_t, sin_t, P["q_gain_col"], P["k_gain"], P["b_gate"],
      P["dt_bias_row"], P["dt_bias_col"], P["conv_w"], P["conv_b"])


DUP = 2 * SSM_HEADS


def _chunk_masks(rev):
    row_i = lax.broadcasted_iota(jnp.int32, (CHUNK, CHUNK), 0)
    col_i = lax.broadcasted_iota(jnp.int32, (CHUNK, CHUNK), 1)
    tri = (col_i >= row_i) if rev else (col_i <= row_i)
    return tri, tri.astype(F32).astype(BF16)


def _split3(x):
    hi = x.astype(BF16)
    r = x - hi.astype(F32)
    mid = r.astype(BF16)
    lo = (r - mid.astype(F32)).astype(BF16)
    return hi, mid, lo


def _ssd_prepare(dt2, dt_t, a2_row, a_col, bm, cm, expand_ref, tri_b, rev):
    tri3 = jnp.concatenate([tri_b, tri_b, tri_b], axis=1)
    acum2 = jnp.dot(tri3, jnp.concatenate(_split3(dt2 * a2_row), axis=0),
                    preferred_element_type=F32)
    acum_t = lax.dot_general(jnp.concatenate(_split3(dt_t * a_col), axis=1), tri3, (((1,), (1,)), ((), ())),
                             preferred_element_type=F32)
    end = 0 if rev else CHUNK - 1
    total2 = acum2[end:end + 1, :]
    ew = jnp.concatenate([jnp.exp2(acum2), jnp.exp2(total2 - acum2) * dt2], axis=0)
    ew_hi = ew.astype(BF16)
    ew_lo = (ew - ew_hi.astype(F32)).astype(BF16)
    lane = lax.broadcasted_iota(jnp.int32, ew.shape, 1)
    rep = jnp.dot(jnp.where(lane < SSM_HEADS, ew_hi, ew_lo), expand_ref[...],
                  preferred_element_type=F32)
    r_t = acum_t - jnp.log2(dt_t)
    b_gs, c_gs, cbs = [], [], []
    for g in range(SSM_GROUPS):
        b_g = bm[:, g * D_STATE:(g + 1) * D_STATE].astype(BF16)
        c_g = cm[:, g * D_STATE:(g + 1) * D_STATE].astype(BF16)
        b_gs.append(b_g)
        c_gs.append(c_g)
        cbs.append(lax.dot_general(c_g, b_g, (((1,), (1,)), ((), ())), preferred_element_type=F32))
    return acum2, r_t, rep[:CHUNK], rep[CHUNK:], b_gs, c_gs, cbs


def _ssd_finish(prep, xs, state_ref, tri, rev):
    acum2, r_t, e_rep, w_rep, b_gs, c_gs, cbs = prep
    end = 0 if rev else CHUNK - 1
    cd_rep = e_rep[end:end + 1, :]
    y_offs = []
    for g in range(SSM_GROUPS):
        cols = slice(g * GROUP_WIDTH, (g + 1) * GROUP_WIDTH)
        prev = state_ref[g]
        y_offs.append(jnp.dot(c_gs[g], prev.astype(BF16), preferred_element_type=F32) * e_rep[:, cols])
        s_new = lax.dot_general(b_gs[g], (xs[:, cols] * w_rep[:, cols]).astype(BF16), (((0,), (0,)), ((), ())),
                                preferred_element_type=F32)
        state_ref[g] = prev * cd_rep[:, cols] + s_new
    lane = lax.broadcasted_iota(jnp.int32, (CHUNK, LANES), 1)
    lo_half = lane < SSM_HEAD_DIM
    ys = []
    for g in range(SSM_GROUPS):
        pairs = []
        for j in range(GROUP_WIDTH // LANES):
            ms = []
            for hh in range(2):
                h = g * (SSM_HEADS // SSM_GROUPS) + 2 * j + hh
                seg = acum2[:, h:h + 1] - r_t[h:h + 1, :]
                ms.append(cbs[g] * jnp.exp2(jnp.where(tri, seg, -jnp.inf)))
            lhs = jnp.concatenate(ms, axis=1).astype(BF16)
            xp = xs[:, g * GROUP_WIDTH + j * LANES:g * GROUP_WIDTH + (j + 1) * LANES]
            rhs = jnp.concatenate([jnp.where(lo_half, xp, 0.0), jnp.where(lo_half, 0.0, xp)],
                                  axis=0).astype(BF16)
            pairs.append(jnp.dot(lhs, rhs, preferred_element_type=F32))
        ys.append(jnp.concatenate(pairs, axis=1) + y_offs[g])
    return jnp.concatenate(ys, axis=1)


def _ssd_scan(xc_ref, dt_ref, dtt_ref, alog_row_ref, alog_col_ref, expand_ref, state_ref, emit, *, nchunk, rev):
    @pl.when(pl.program_id(1) == 0)
    def _():
        state_ref[...] = jnp.zeros_like(state_ref)

    lo, lo_t = (DUP, SSM_HEADS) if rev else (0, 0)
    a2_row = -jnp.exp(alog_row_ref[:, lo:lo + DUP]) * LOG2_E
    a_col = -jnp.exp(alog_col_ref[lo_t:lo_t + SSM_HEADS, :]) * LOG2_E
    tri, tri_b = _chunk_masks(rev)

    def prepare(c):
        rows = slice(c * CHUNK, (c + 1) * CHUNK)
        return _ssd_prepare(dt_ref[rows, lo:lo + DUP], dtt_ref[lo_t:lo_t + SSM_HEADS, rows], a2_row, a_col,
                            xc_ref[rows, D_INNER:D_INNER + BC_WIDTH], xc_ref[rows, D_INNER + BC_WIDTH:],
                            expand_ref, tri_b, rev)

    order = list(range(nchunk))[::-1] if rev else list(range(nchunk))
    nxt = prepare(order[0])
    for i, c in enumerate(order):
        cur = nxt
        if i + 1 < nchunk:
            nxt = prepare(order[i + 1])
        rows = slice(c * CHUNK, (c + 1) * CHUNK)
        xs = xc_ref[rows, :D_INNER]
        emit(rows, xs, _ssd_finish(cur, xs, state_ref, tri, rev))


def _ssd_fwd_kernel(xc_ref, dt_ref, dtt_ref, alog_row_ref, alog_col_ref, expand_ref, y_ref, state_ref, *, nchunk):
    def emit(rows, xs, y):
        y_ref[rows, :] = y

    _ssd_scan(xc_ref, dt_ref, dtt_ref, alog_row_ref, alog_col_ref, expand_ref, state_ref, emit,
              nchunk=nchunk, rev=False)


def _ssd_bwd_kernel(xc_ref, dt_ref, dtt_ref, alog_row_ref, alog_col_ref, expand_ref, yf_ref, z_ref, dskip_ref,
                    gain_ref, s_ref, state_ref, *, nchunk):
    def emit(rows, xs, y_b):
        y = (yf_ref[rows, :] + y_b + dskip_ref[...] * xs) * _silu(z_ref[rows, :])
        for g in range(SSM_GROUPS):
            cols = slice(g * GROUP_WIDTH, (g + 1) * GROUP_WIDTH)
            s_ref[rows, cols] = _rms(y[:, cols], gain_ref[:, cols]).astype(s_ref.dtype)

    _ssd_scan(xc_ref, dt_ref, dtt_ref, alog_row_ref, alog_col_ref, expand_ref, state_ref, emit,
              nchunk=nchunk, rev=True)


def _ssd(xc, dt, dt_t, z, P, *, nchunk=4):
    b, L, _ = xc.shape
    tl = nchunk * CHUNK
    nb = L // tl

    def tok(width, rev):
        if rev:
            return pl.BlockSpec((None, tl, width), lambda bi, c: (bi, nb - 1 - c, 0))
        return pl.BlockSpec((None, tl, width), lambda bi, c: (bi, c, 0))

    def tok_t(rev):
        if rev:
            return pl.BlockSpec((None, DT_PAD, tl), lambda bi, c: (bi, 0, nb - 1 - c))
        return pl.BlockSpec((None, DT_PAD, tl), lambda bi, c: (bi, 0, c))

    state = pltpu.VMEM((SSM_GROUPS, D_STATE, GROUP_WIDTH), F32)
    consts = [_resident((1, DT_PAD)), _resident((DT_PAD, 1)), _resident((DUP, D_INNER))]
    y_f = pl.pallas_call(
        functools.partial(_ssd_fwd_kernel, nchunk=nchunk),
        grid=(b, nb),
        in_specs=[tok(CONV_DIM, False), tok(DT_PAD, False), tok_t(False)] + consts,
        out_specs=tok(D_INNER, False),
        out_shape=jax.ShapeDtypeStruct((b, L, D_INNER), F32),
        scratch_shapes=[state],
        compiler_params=_params("parallel", "arbitrary"),
        name="ssd_fwd",
    )(xc, dt, dt_t, P["a_log_row"], P["a_log_col"], P["head_expand"])
    return pl.pallas_call(
        functools.partial(_ssd_bwd_kernel, nchunk=nchunk),
        grid=(b, nb),
        in_specs=[tok(CONV_DIM, True), tok(DT_PAD, True), tok_t(True)] + consts
                 + [tok(D_INNER, True), tok(D_INNER, True), _resident((1, D_INNER)), _resident((1, D_INNER))],
        out_specs=tok(D_INNER, True),
        out_shape=jax.ShapeDtypeStruct((b, L, D_INNER), BF16),
        scratch_shapes=[state],
        compiler_params=_params("parallel", "arbitrary"),
        name="ssd_bwd",
    )(xc, dt, dt_t, P["a_log_row"], P["a_log_col"], P["head_expand"], y_f, z, P["d_skip_rep"], P["ssm_norm"])


ONES_ROWS = 16
QK_LOOKAHEAD = 3


def _attn_kernel(qt_ref, k_ref, vt_ref, o_ref, m_ref, acc_ref, *, tq, nb):
    kv = pl.program_id(3)

    @pl.when(kv == 0)
    def _():
        m_ref[...] = jnp.full_like(m_ref, -jnp.inf)
        acc_ref[...] = jnp.zeros_like(acc_ref)

    k = k_ref[...]
    vt = vt_ref[...]
    vt_aug = jnp.concatenate([vt, jnp.ones((ONES_ROWS, vt.shape[1]), BF16)], axis=0)
    blocks = [(g, c) for g in range(Q_PER_KV) for c in range(tq // nb)]

    def scores(g, c):
        qt = qt_ref[g * HEAD_DIM:(g + 1) * HEAD_DIM, c * nb:(c + 1) * nb]
        return jnp.dot(k, qt, preferred_element_type=F32)

    pending = [scores(*blk) for blk in blocks[:QK_LOOKAHEAD]]
    for i, (g, c) in enumerate(blocks):
        s = pending.pop(0)
        if i + QK_LOOKAHEAD < len(blocks):
            pending.append(scores(*blocks[i + QK_LOOKAHEAD]))
        cols = slice(g * tq + c * nb, g * tq + (c + 1) * nb)
        m_prev = m_ref[:, cols]
        m_new = jnp.maximum(m_prev, jnp.max(s, axis=0, keepdims=True))
        alpha = jnp.exp2(m_prev - m_new)
        p = jnp.exp2(s - m_new).astype(BF16)
        acc_ref[:, cols] = alpha * acc_ref[:, cols] + jnp.dot(vt_aug, p, preferred_element_type=F32)
        m_ref[:, cols] = m_new

    @pl.when(kv == pl.num_programs(3) - 1)
    def _():
        for g in range(Q_PER_KV):
            acc = acc_ref[:, g * tq:(g + 1) * tq]
            out_t = acc[:HEAD_DIM] / acc[HEAD_DIM:HEAD_DIM + 1]
            o_ref[:, g * HEAD_DIM:(g + 1) * HEAD_DIM] = out_t.T.astype(o_ref.dtype)


def _attn_bounded_kernel(qt_ref, k_ref, vt_ref, o_ref, acc_ref, l_ref, *, tq, nb, kb):
    kv = pl.program_id(3)

    @pl.when(kv == 0)
    def _():
        acc_ref[...] = jnp.zeros_like(acc_ref)
        l_ref[...] = jnp.zeros_like(l_ref)

    tk = k_ref.shape[0]
    blocks = [(t, g, c) for t in range(tk // kb) for g in range(Q_PER_KV) for c in range(tq // nb)]

    def scores(t, g, c):
        qt = qt_ref[g * HEAD_DIM:(g + 1) * HEAD_DIM, c * nb:(c + 1) * nb]
        return jnp.dot(k_ref[t * kb:(t + 1) * kb, :], qt, preferred_element_type=F32)

    pending = [scores(*blk) for blk in blocks[:QK_LOOKAHEAD]]
    for i, (t, g, c) in enumerate(blocks):
        s = pending.pop(0)
        if i + QK_LOOKAHEAD < len(blocks):
            pending.append(scores(*blocks[i + QK_LOOKAHEAD]))
        cols = slice(g * tq + c * nb, g * tq + (c + 1) * nb)
        p = jnp.exp2(s)
        l_ref[:, cols] += jnp.sum(p.reshape(kb // SUBLANES, SUBLANES, nb), axis=0)
        acc_ref[:, cols] += jnp.dot(vt_ref[:, t * kb:(t + 1) * kb], p.astype(BF16), preferred_element_type=F32)

    @pl.when(kv == pl.num_programs(3) - 1)
    def _():
        for g in range(Q_PER_KV):
            cols = slice(g * tq, (g + 1) * tq)
            out_t = acc_ref[:, cols] / jnp.sum(l_ref[:, cols], axis=0, keepdims=True)
            o_ref[:, g * HEAD_DIM:(g + 1) * HEAD_DIM] = out_t.T.astype(o_ref.dtype)


def _attention(qt, k, vt, *, bounded, tq, tk, nb=256):
    b, L, _ = k.shape
    gw = Q_PER_KV * HEAD_DIM
    if bounded:
        body = functools.partial(_attn_bounded_kernel, kb=min(tk, 512))
        scratch = [pltpu.VMEM((HEAD_DIM, Q_PER_KV * tq), F32), pltpu.VMEM((SUBLANES, Q_PER_KV * tq), F32)]
    else:
        body = _attn_kernel
        scratch = [pltpu.VMEM((1, Q_PER_KV * tq), F32), pltpu.VMEM((HEAD_DIM + ONES_ROWS, Q_PER_KV * tq), F32)]
    return pl.pallas_call(
        functools.partial(body, tq=tq, nb=nb),
        grid=(b, N_KV_HEADS, L // tq, L // tk),
        in_specs=[pl.BlockSpec((None, gw, tq), lambda bi, h, i, j: (bi, h, i)),
                  pl.BlockSpec((None, tk, HEAD_DIM), lambda bi, h, i, j: (bi, j, h)),
                  pl.BlockSpec((None, HEAD_DIM, tk), lambda bi, h, i, j: (bi, h, j))],
        out_specs=pl.BlockSpec((None, tq, gw), lambda bi, h, i, j: (bi, i, h)),
        out_shape=jax.ShapeDtypeStruct((b, L, ATTN_WIDTH), BF16),
        scratch_shapes=scratch,
        compiler_params=_params("parallel", "parallel", "parallel", "arbitrary"),
        name="attn_bounded" if bounded else "attn",
    )(qt, k, vt)


SCORE_BOUND = 40.0
_BF16_NORM_SLACK = (1.0 + 2.0 ** -8) ** 2


def _attention_dispatch(qt, k, vt, q_gain, k_gain):
    L = k.shape[1]
    bound = (LOG2_E * HEAD_DIM ** 0.5 * _BF16_NORM_SLACK) * jnp.max(jnp.abs(q_gain)) * jnp.max(jnp.abs(k_gain))
    tq_fast = next(t for t in (2048, 1024, 512) if L % t == 0)
    return lax.cond(bound <= SCORE_BOUND,
                    functools.partial(_attention, bounded=True, tq=tq_fast, tk=min(tq_fast, 1024)),
                    functools.partial(_attention, bounded=False, tq=512, tk=512),
                    qt, k, vt)


def _merge_kernel(s_ref, a_ref, g_ref, h_ref, ws_ref, wa_ref, wo_ref, o_ref):
    ps = jnp.dot(s_ref[...], ws_ref[...], preferred_element_type=F32)
    pa = jnp.dot(a_ref[...], wa_ref[...], preferred_element_type=F32)
    m = g_ref[:, :D_MODEL] * ps + g_ref[:, D_MODEL:] * pa
    o_ref[...] = h_ref[...] + jnp.dot(m.astype(BF16), wo_ref[...], preferred_element_type=F32)


def _merge(s, a, g, h, P, *, tm=512):
    b, L, _ = h.shape

    def tok(width):
        return pl.BlockSpec((None, tm, width), lambda bi, i: (bi, i, 0))

    return pl.pallas_call(
        _merge_kernel,
        grid=(b, L // tm),
        in_specs=[tok(D_INNER), tok(ATTN_WIDTH), tok(2 * D_MODEL), tok(D_MODEL),
                  _resident((D_INNER, D_MODEL)), _resident((ATTN_WIDTH, D_MODEL)),
                  _resident((D_MODEL, D_MODEL))],
        out_specs=tok(D_MODEL),
        out_shape=jax.ShapeDtypeStruct((b, L, D_MODEL), F32),
        compiler_params=_params("parallel", "parallel"),
        name="merge",
    )(s, a, g, h, P["w_ssm_branch"], P["w_attn_branch"], P["w_out"])


def _rope_tables(L):
    rows = L // GRID_W
    row = jnp.repeat(jnp.arange(rows, dtype=F32), GRID_W)
    col = jnp.tile(jnp.arange(GRID_W, dtype=F32), rows)
    axis_dim = HEAD_DIM // 2
    inv_freq = ROPE_THETA ** (-jnp.arange(0, axis_dim, 2, dtype=F32) / axis_dim)
    ang = jnp.concatenate([row[:, None] * inv_freq, col[:, None] * inv_freq], axis=-1)
    cos, sin = jnp.cos(ang), jnp.sin(ang)
    cos2, sin2 = jnp.concatenate([cos, cos], axis=-1), jnp.concatenate([-sin, sin], axis=-1)
    return cos2, sin2, cos2.T, sin2.T


def _layer_params(norm_ffn1, w_ffn1_gu, w_ffn1_down, norm_mix, w_in, conv_w, conv_b, dt_bias_f, dt_bias_b,
                  A_log_f, A_log_b, D_skip, ssm_norm, q_norm, k_norm, w_ssm_branch, w_attn_branch, b_gate,
                  w_out, norm_ffn2, w_ffn2_gu, w_ffn2_down):
    sizes = (D_INNER, CONV_DIM, SSM_HEADS, SSM_HEADS, ATTN_WIDTH, KV_WIDTH, KV_WIDTH, 2 * D_MODEL)
    splits = [int(v) for v in np.cumsum(sizes)[:-1]]
    w_z, w_xbc, w_dtf, w_dtb, w_q, w_k, w_v, w_g = jnp.split(w_in, splits, axis=-1)
    perm = np.concatenate([np.arange(0, HEAD_DIM, 2), np.arange(1, HEAD_DIM, 2)])
    q_cols = np.concatenate([h * HEAD_DIM + perm for h in range(N_Q_HEADS)])
    k_cols = np.concatenate([h * HEAD_DIM + perm for h in range(N_KV_HEADS)])
    w_dt_dup = jnp.concatenate([w_dtf, w_dtf, w_dtb, w_dtb], axis=-1)
    w_dt_pad = jnp.concatenate([w_dtf, w_dtb, jnp.zeros((D_MODEL, DT_PAD - 2 * SSM_HEADS), F32)], axis=-1)
    w_r = jnp.concatenate([w_z, w_xbc, w_k[:, k_cols], w_g, w_dt_dup], axis=-1)
    w_t = jnp.concatenate([w_q[:, q_cols], w_v, w_dt_pad], axis=-1).T
    pad = jnp.zeros((DT_PAD - 2 * SSM_HEADS,), F32)
    head_expand = jnp.repeat(jnp.tile(jnp.eye(SSM_HEADS, dtype=BF16), (2, 1)), SSM_HEAD_DIM, axis=1)
    row = lambda v: v.reshape(1, -1).astype(F32)
    return dict(
        norm_ffn1=row(norm_ffn1), w_ffn1_gu=w_ffn1_gu.astype(BF16), w_ffn1_down=w_ffn1_down.astype(BF16),
        norm_mix=row(norm_mix), w_in=w_r.astype(BF16), w_in_t=w_t.astype(BF16),
        conv_w=jnp.concatenate([conv_w, jnp.zeros((SUBLANES - D_CONV, CONV_DIM), F32)], axis=0),
        conv_b=row(conv_b),
        dt_bias_row=row(jnp.concatenate([dt_bias_f, dt_bias_f, dt_bias_b, dt_bias_b])),
        dt_bias_col=jnp.concatenate([dt_bias_f, dt_bias_b, pad]).reshape(-1, 1),
        a_log_row=row(jnp.concatenate([A_log_f, A_log_f, A_log_b, A_log_b])),
        a_log_col=jnp.concatenate([A_log_f, A_log_b, pad]).reshape(-1, 1), head_expand=head_expand,
        d_skip_rep=row(jnp.repeat(D_skip, SSM_HEAD_DIM)), ssm_norm=row(ssm_norm),
        q_gain_col=q_norm[perm].reshape(-1, 1).astype(F32), k_gain=row(k_norm[perm]),
        w_ssm_branch=w_ssm_branch.astype(BF16), w_attn_branch=w_attn_branch.astype(BF16),
        b_gate=row(b_gate), w_out=w_out.astype(BF16),
        norm_ffn2=row(norm_ffn2), w_ffn2_gu=w_ffn2_gu.astype(BF16), w_ffn2_down=w_ffn2_down.astype(BF16),
    )


def _encoder_layer(x, P, g_next, last):
    L = x.shape[1]
    h, u = _ffn(x, P["norm_ffn1"], P["w_ffn1_gu"], P["w_ffn1_down"], P["norm_mix"],
                emit_h=True, post_dtype=BF16)
    z, xc, q_t, k, v_t, g, dt, dt_t = _inproj(u, P, _rope_tables(L))
    s = _ssd(xc, dt, dt_t, z, P)
    a = _attention_dispatch(q_t, k, v_t, P["q_gain_col"], P["k_gain"])
    h2 = _merge(s, a, g, h, P)
    outs = _ffn(h2, P["norm_ffn2"], P["w_ffn2_gu"], P["w_ffn2_down"], g_next,
                emit_h=not last, post_dtype=F32)
    return outs


def _trunk(x, layers, norm_final):
    depth = len(layers)
    for i, P in enumerate(layers):
        last = i == depth - 1
        g_next = norm_final.reshape(1, -1) if last else layers[i + 1]["norm_ffn1"]
        outs = _encoder_layer(x, P, g_next, last)
        x = outs[0]
    return x


def kernel(x_prompt, x_sample, norm_ffn1, w_ffn1_gu, w_ffn1_down, norm_mix, w_in, conv_w, conv_b, dt_bias_f,
           dt_bias_b, A_log_f, A_log_b, D_skip, ssm_norm, q_norm, k_norm, w_ssm_branch, w_attn_branch, b_gate,
           w_out, norm_ffn2, w_ffn2_gu, w_ffn2_down, norm_final):
    per_layer = (norm_ffn1, w_ffn1_gu, w_ffn1_down, norm_mix, w_in, conv_w, conv_b, dt_bias_f, dt_bias_b,
                 A_log_f, A_log_b, D_skip, ssm_norm, q_norm, k_norm, w_ssm_branch, w_attn_branch, b_gate,
                 w_out, norm_ffn2, w_ffn2_gu, w_ffn2_down)
    depth = norm_ffn1.shape[0]
    layers = [_layer_params(*(w[i] for w in per_layer)) for i in range(depth)]
    return (_trunk(x_prompt, layers, norm_final), _trunk(x_sample, layers, norm_final))
```

```python
import functools

import jax
import jax.numpy as jnp
import numpy as np
from jax import lax
from jax.experimental import pallas as pl
from jax.experimental.pallas import tpu as pltpu

F32 = jnp.float32
BF16 = jnp.bfloat16

D_MODEL = 1024
GRID_W = 64
D_INNER = 2048
SSM_HEAD_DIM = 64
SSM_HEADS = 32
SSM_GROUPS = 4
GROUP_WIDTH = D_INNER // SSM_GROUPS
D_STATE = 128
D_CONV = 5
CHUNK = 128
BC_WIDTH = SSM_GROUPS * D_STATE
CONV_DIM = D_INNER + 2 * BC_WIDTH
HEAD_DIM = 128
N_Q_HEADS = 8
N_KV_HEADS = 2
Q_PER_KV = N_Q_HEADS // N_KV_HEADS
ATTN_WIDTH = N_Q_HEADS * HEAD_DIM
KV_WIDTH = N_KV_HEADS * HEAD_DIM
ROPE_THETA = 10000.0
D_FF = 2816
EPS = 1e-6
DT_PAD = 128

LANES = 128
SUBLANES = 8
VMEM_LIMIT = 56 * 1024 * 1024

LOG2_E = 1.4426950408889634

_OFF_Z = 0
_OFF_XBC = _OFF_Z + D_INNER
_OFF_K = _OFF_XBC + CONV_DIM
_OFF_G = _OFF_K + KV_WIDTH
_OFF_DT = _OFF_G + 2 * D_MODEL
_IN_COLS = _OFF_DT + DT_PAD
_ROW_Q = 0
_ROW_V = _ROW_Q + ATTN_WIDTH
_ROW_DT = _ROW_V + KV_WIDTH
_T_ROWS = _ROW_DT + DT_PAD


def _rms(x, g):
    return x * lax.rsqrt(jnp.mean(x * x, axis=-1, keepdims=True) + EPS) * g


def _silu(x):
    return x * jax.nn.sigmoid(x)


def _softplus(x):
    return jnp.maximum(x, 0.0) + jnp.log1p(jnp.exp(-jnp.abs(x)))


def _params(*sem):
    return pltpu.CompilerParams(dimension_semantics=sem, vmem_limit_bytes=VMEM_LIMIT)


def _resident(shape):
    nd = len(shape)
    return pl.BlockSpec(shape, lambda *_: (0,) * nd, pipeline_mode=pl.Buffered(1))


def _ffn_kernel(x_ref, gpre_ref, wgu_ref, wd_ref, gpost_ref, *out_refs, tf, emit_h):
    x = x_ref[...]
    xn = _rms(x, gpre_ref[...]).astype(BF16)
    acc = jnp.zeros_like(x)
    for c in range(D_FF // tf):
        g = jnp.dot(xn, wgu_ref[:, c * tf:(c + 1) * tf], preferred_element_type=F32)
        u = jnp.dot(xn, wgu_ref[:, D_FF + c * tf:D_FF + (c + 1) * tf], preferred_element_type=F32)
        act = (_silu(g) * u).astype(BF16)
        acc = acc + jnp.dot(act, wd_ref[c * tf:(c + 1) * tf, :], preferred_element_type=F32)
    h = x + 0.5 * acc
    if emit_h:
        out_refs[0][...] = h
    post_ref = out_refs[-1]
    post_ref[...] = _rms(h, gpost_ref[...]).astype(post_ref.dtype)


def _ffn(x, g_pre, w_gu, w_down, g_post, *, emit_h, post_dtype, tm=512, tf=256):
    b, L, _ = x.shape
    tok = pl.BlockSpec((None, tm, D_MODEL), lambda bi, i: (bi, i, 0))
    out_shape = [jax.ShapeDtypeStruct((b, L, D_MODEL), post_dtype)]
    out_specs = [tok]
    if emit_h:
        out_shape.insert(0, jax.ShapeDtypeStruct((b, L, D_MODEL), F32))
        out_specs.insert(0, tok)
    return pl.pallas_call(
        functools.partial(_ffn_kernel, tf=tf, emit_h=emit_h),
        grid=(b, L // tm),
        in_specs=[tok, _resident((1, D_MODEL)), _resident((D_MODEL, 2 * D_FF)),
                  _resident((D_FF, D_MODEL)), _resident((1, D_MODEL))],
        out_specs=out_specs,
        out_shape=out_shape,
        compiler_params=_params("parallel", "parallel"),
        name="ffn",
    )(x, g_pre, w_gu, w_down, g_post)


CONV_CHUNK = 512
U_HALO = 16


def _inproj_kernel(u_ref, uprev_ref, unext_ref, w_ref, wt_ref, cos_ref, sin_ref, cos_t_ref, sin_t_ref, qg_ref,
                   kg_ref, bg_ref, dtb_row_ref, dtb_col_ref, cw_ref, cb_ref,
                   z_ref, xc_ref, qt_ref, k_ref, vt_ref, g_ref, dt_ref, dtt_ref, ext_ref):
    u = u_ref[...]
    tm = u.shape[0]

    def proj(off, width):
        return jnp.dot(u, w_ref[:, off:off + width], preferred_element_type=F32)

    def proj_t(off, rows):
        return lax.dot_general(wt_ref[off:off + rows, :], u, (((1,), (1,)), ((), ())),
                               preferred_element_type=F32)

    i = pl.program_id(1)
    u_ext = jnp.concatenate([uprev_ref[...], u, unext_ref[...]], axis=0)
    first = U_HALO - D_CONV // 2

    def project_chunk(c):
        cols = slice(c * CONV_CHUNK, (c + 1) * CONV_CHUNK)
        x = jnp.dot(u_ext, w_ref[:, _OFF_XBC + c * CONV_CHUNK:_OFF_XBC + (c + 1) * CONV_CHUNK],
                    preferred_element_type=F32)
        ext_ref[0:U_HALO, cols] = jnp.where(i > 0, x[0:U_HALO], 0.0)
        ext_ref[U_HALO:U_HALO + tm, cols] = x[U_HALO:U_HALO + tm]
        ext_ref[U_HALO + tm:, cols] = jnp.where(i < pl.num_programs(1) - 1, x[U_HALO + tm:], 0.0)

    def conv_chunk(c):
        cols = slice(c * CONV_CHUNK, (c + 1) * CONV_CHUNK)
        acc = jnp.zeros((tm, CONV_CHUNK), F32) + cb_ref[:, cols]
        for j in range(D_CONV):
            acc = acc + ext_ref[first + j:first + j + tm, cols] * cw_ref[j:j + 1, cols]
        xc_ref[:, cols] = _silu(acc)

    n_chunks = CONV_DIM // CONV_CHUNK
    project_chunk(0)
    for c in range(n_chunks):
        if c + 1 < n_chunks:
            project_chunk(c + 1)
        conv_chunk(c)
    z_ref[...] = proj(_OFF_Z, D_INNER)
    g_ref[...] = jax.nn.sigmoid(proj(_OFF_G, 2 * D_MODEL) + bg_ref[...])
    dt_ref[...] = _softplus(proj(_OFF_DT, DT_PAD) + dtb_row_ref[...])
    dtt_ref[...] = _softplus(proj_t(_ROW_DT, DT_PAD) + dtb_col_ref[...])
    vt_ref[...] = proj_t(_ROW_V, KV_WIDTH).astype(BF16)

    half = HEAD_DIM // 2
    cos = cos_ref[...]
    sin = sin_ref[...]
    for h in range(N_KV_HEADS):
        x = _rms(proj(_OFF_K + h * HEAD_DIM, HEAD_DIM), kg_ref[...])
        k_ref[:, h * HEAD_DIM:(h + 1) * HEAD_DIM] = (x * cos + pltpu.roll(x, half, axis=1) * sin).astype(BF16)

    cos_t = cos_t_ref[...]
    sin_t = sin_t_ref[...]
    q_scale = HEAD_DIM ** -0.5 * LOG2_E
    for h in range(N_Q_HEADS):
        x = proj_t(_ROW_Q + h * HEAD_DIM, HEAD_DIM)
        x = x * lax.rsqrt(jnp.mean(x * x, axis=0, keepdims=True) + EPS) * qg_ref[...]
        rot = jnp.concatenate([x[half:], x[:half]], axis=0)
        qt_ref[h * HEAD_DIM:(h + 1) * HEAD_DIM, :] = ((x * cos_t + rot * sin_t) * q_scale).astype(BF16)


def _inproj(u, P, tables, *, tm=256):
    b, L, _ = u.shape

    def tok(width):
        return pl.BlockSpec((None, tm, width), lambda bi, i: (bi, i, 0))

    def tok_t(rows):
        return pl.BlockSpec((None, rows, tm), lambda bi, i: (bi, 0, i))

    pos = pl.BlockSpec((tm, HEAD_DIM), lambda bi, i: (i, 0))
    pos_t = pl.BlockSpec((HEAD_DIM, tm), lambda bi, i: (0, i))
    outs = [(tok, D_INNER, F32), (tok, CONV_DIM, F32), (tok_t, ATTN_WIDTH, BF16), (tok, KV_WIDTH, BF16),
            (tok_t, KV_WIDTH, BF16), (tok, 2 * D_MODEL, F32), (tok, DT_PAD, F32), (tok_t, DT_PAD, F32)]
    out_shape = [jax.ShapeDtypeStruct((b, L, w) if mk is tok else (b, w, L), dt) for mk, w, dt in outs]
    out_specs = [mk(w) for mk, w, _ in outs]
    cos, sin, cos_t, sin_t = tables
    r = tm // U_HALO
    last = L // U_HALO - 1
    halo_prev = pl.BlockSpec((None, U_HALO, D_MODEL), lambda bi, i: (bi, jnp.maximum(i * r - 1, 0), 0))
    halo_next = pl.BlockSpec((None, U_HALO, D_MODEL), lambda bi, i: (bi, jnp.minimum((i + 1) * r, last), 0))
    return pl.pallas_call(
        _inproj_kernel,
        grid=(b, L // tm),
        in_specs=[tok(D_MODEL), halo_prev, halo_next, _resident((D_MODEL, _IN_COLS)),
                  _resident((_T_ROWS, D_MODEL)), pos, pos,
                  pos_t, pos_t, _resident((HEAD_DIM, 1)), _resident((1, HEAD_DIM)), _resident((1, 2 * D_MODEL)),
                  _resident((1, DT_PAD)), _resident((DT_PAD, 1)),
                  _resident((SUBLANES, CONV_DIM)), _resident((1, CONV_DIM))],
        out_specs=out_specs,
        out_shape=out_shape,
        scratch_shapes=[pltpu.VMEM((tm + 2 * U_HALO, CONV_DIM), F32)],
        compiler_params=_params("parallel", "parallel"),
        name="in_proj",
    )(u, u, u, P["w_in"], P["w_in_t"], cos, sin, cos_t, sin_t, P["q_gain_col"], P["k_gain"], P["b_gate"],
      P["dt_bias_row"], P["dt_bias_col"], P["conv_w"], P["conv_b"])


DUP = 2 * SSM_HEADS


def _chunk_masks(rev):
    row_i = lax.broadcasted_iota(jnp.int32, (CHUNK, CHUNK), 0)
    col_i = lax.broadcasted_iota(jnp.int32, (CHUNK, CHUNK), 1)
    tri = (col_i >= row_i) if rev else (col_i <= row_i)
    return tri, tri.astype(F32).astype(BF16)


def _split3(x):
    hi = x.astype(BF16)
    r = x - hi.astype(F32)
    mid = r.astype(BF16)
    lo = (r - mid.astype(F32)).astype(BF16)
    return hi, mid, lo


def _ssd_prepare(dt2, dt_t, a2_row, a_col, bm, cm, expand_ref, tri_b, rev):
    tri3 = jnp.concatenate([tri_b, tri_b, tri_b], axis=1)
    acum2 = jnp.dot(tri3, jnp.concatenate(_split3(dt2 * a2_row), axis=0),
                    preferred_element_type=F32)
    acum_t = lax.dot_general(jnp.concatenate(_split3(dt_t * a_col), axis=1), tri3, (((1,), (1,)), ((), ())),
                             preferred_element_type=F32)
    end = 0 if rev else CHUNK - 1
    total2 = acum2[end:end + 1, :]
    ew = jnp.concatenate([jnp.exp2(acum2), jnp.exp2(total2 - acum2) * dt2], axis=0)
    ew_hi = ew.astype(BF16)
    ew_lo = (ew - ew_hi.astype(F32)).astype(BF16)
    lane = lax.broadcasted_iota(jnp.int32, ew.shape, 1)
    rep = jnp.dot(jnp.where(lane < SSM_HEADS, ew_hi, ew_lo), expand_ref[...],
                  preferred_element_type=F32)
    r_t = acum_t - jnp.log2(dt_t)
    b_gs, c_gs, cbs = [], [], []
    for g in range(SSM_GROUPS):
        b_g = bm[:, g * D_STATE:(g + 1) * D_STATE].astype(BF16)
        c_g = cm[:, g * D_STATE:(g + 1) * D_STATE].astype(BF16)
        b_gs.append(b_g)
        c_gs.append(c_g)
        cbs.append(lax.dot_general(c_g, b_g, (((1,), (1,)), ((), ())), preferred_element_type=F32))
    return acum2, r_t, rep[:CHUNK], rep[CHUNK:], b_gs, c_gs, cbs


def _ssd_finish(prep, xs, state_ref, tri, rev):
    acum2, r_t, e_rep, w_rep, b_gs, c_gs, cbs = prep
    end = 0 if rev else CHUNK - 1
    cd_rep = e_rep[end:end + 1, :]
    y_offs = []
    for g in range(SSM_GROUPS):
        cols = slice(g * GROUP_WIDTH, (g + 1) * GROUP_WIDTH)
        prev = state_ref[g]
        y_offs.append(jnp.dot(c_gs[g], prev.astype(BF16), preferred_element_type=F32) * e_rep[:, cols])
        s_new = lax.dot_general(b_gs[g], (xs[:, cols] * w_rep[:, cols]).astype(BF16), (((0,), (0,)), ((), ())),
                                preferred_element_type=F32)
        state_ref[g] = prev * cd_rep[:, cols] + s_new
    lane = lax.broadcasted_iota(jnp.int32, (CHUNK, LANES), 1)
    lo_half = lane < SSM_HEAD_DIM
    ys = []
    for g in range(SSM_GROUPS):
        pairs = []
        for j in range(GROUP_WIDTH // LANES):
            ms = []
            for hh in range(2):
                h = g * (SSM_HEADS // SSM_GROUPS) + 2 * j + hh
                seg = acum2[:, h:h + 1] - r_t[h:h + 1, :]
                ms.append(cbs[g] * jnp.exp2(jnp.where(tri, seg, -jnp.inf)))
            lhs = jnp.concatenate(ms, axis=1).astype(BF16)
            xp = xs[:, g * GROUP_WIDTH + j * LANES:g * GROUP_WIDTH + (j + 1) * LANES]
            rhs = jnp.concatenate([jnp.where(lo_half, xp, 0.0), jnp.where(lo_half, 0.0, xp)],
                                  axis=0).astype(BF16)
            pairs.append(jnp.dot(lhs, rhs, preferred_element_type=F32))
        ys.append(jnp.concatenate(pairs, axis=1) + y_offs[g])
    return jnp.concatenate(ys, axis=1)


def _ssd_scan(xc_ref, dt_ref, dtt_ref, alog_row_ref, alog_col_ref, expand_ref, state_ref, emit, *, nchunk, rev):
    @pl.when(pl.program_id(1) == 0)
    def _():
        state_ref[...] = jnp.zeros_like(state_ref)

    lo, lo_t = (DUP, SSM_HEADS) if rev else (0, 0)
    a2_row = -jnp.exp(alog_row_ref[:, lo:lo + DUP]) * LOG2_E
    a_col = -jnp.exp(alog_col_ref[lo_t:lo_t + SSM_HEADS, :]) * LOG2_E
    tri, tri_b = _chunk_masks(rev)

    def prepare(c):
        rows = slice(c * CHUNK, (c + 1) * CHUNK)
        return _ssd_prepare(dt_ref[rows, lo:lo + DUP], dtt_ref[lo_t:lo_t + SSM_HEADS, rows], a2_row, a_col,
                            xc_ref[rows, D_INNER:D_INNER + BC_WIDTH], xc_ref[rows, D_INNER + BC_WIDTH:],
                            expand_ref, tri_b, rev)

    order = list(range(nchunk))[::-1] if rev else list(range(nchunk))
    nxt = prepare(order[0])
    for i, c in enumerate(order):
        cur = nxt
        if i + 1 < nchunk:
            nxt = prepare(order[i + 1])
        rows = slice(c * CHUNK, (c + 1) * CHUNK)
        xs = xc_ref[rows, :D_INNER]
        emit(rows, xs, _ssd_finish(cur, xs, state_ref, tri, rev))


def _ssd_fwd_kernel(xc_ref, dt_ref, dtt_ref, alog_row_ref, alog_col_ref, expand_ref, y_ref, state_ref, *, nchunk):
    def emit(rows, xs, y):
        y_ref[rows, :] = y

    _ssd_scan(xc_ref, dt_ref, dtt_ref, alog_row_ref, alog_col_ref, expand_ref, state_ref, emit,
              nchunk=nchunk, rev=False)


def _ssd_bwd_kernel(xc_ref, dt_ref, dtt_ref, alog_row_ref, alog_col_ref, expand_ref, yf_ref, z_ref, dskip_ref,
                    gain_ref, s_ref, state_ref, *, nchunk):
    def emit(rows, xs, y_b):
        y = (yf_ref[rows, :] + y_b + dskip_ref[...] * xs) * _silu(z_ref[rows, :])
        for g in range(SSM_GROUPS):
            cols = slice(g * GROUP_WIDTH, (g + 1) * GROUP_WIDTH)
            s_ref[rows, cols] = _rms(y[:, cols], gain_ref[:, cols]).astype(s_ref.dtype)

    _ssd_scan(xc_ref, dt_ref, dtt_ref, alog_row_ref, alog_col_ref, expand_ref, state_ref, emit,
              nchunk=nchunk, rev=True)


def _ssd(xc, dt, dt_t, z, P, *, nchunk=4):
    b, L, _ = xc.shape
    tl = nchunk * CHUNK
    nb = L // tl

    def tok(width, rev):
        if rev:
            return pl.BlockSpec((None, tl, width), lambda bi, c: (bi, nb - 1 - c, 0))
        return pl.BlockSpec((None, tl, width), lambda bi, c: (bi, c, 0))

    def tok_t(rev):
        if rev:
            return pl.BlockSpec((None, DT_PAD, tl), lambda bi, c: (bi, 0, nb - 1 - c))
        return pl.BlockSpec((None, DT_PAD, tl), lambda bi, c: (bi, 0, c))

    state = pltpu.VMEM((SSM_GROUPS, D_STATE, GROUP_WIDTH), F32)
    consts = [_resident((1, DT_PAD)), _resident((DT_PAD, 1)), _resident((DUP, D_INNER))]
    y_f = pl.pallas_call(
        functools.partial(_ssd_fwd_kernel, nchunk=nchunk),
        grid=(b, nb),
        in_specs=[tok(CONV_DIM, False), tok(DT_PAD, False), tok_t(False)] + consts,
        out_specs=tok(D_INNER, False),
        out_shape=jax.ShapeDtypeStruct((b, L, D_INNER), F32),
        scratch_shapes=[state],
        compiler_params=_params("parallel", "arbitrary"),
        name="ssd_fwd",
    )(xc, dt, dt_t, P["a_log_row"], P["a_log_col"], P["head_expand"])
    return pl.pallas_call(
        functools.partial(_ssd_bwd_kernel, nchunk=nchunk),
        grid=(b, nb),
        in_specs=[tok(CONV_DIM, True), tok(DT_PAD, True), tok_t(True)] + consts
                 + [tok(D_INNER, True), tok(D_INNER, True), _resident((1, D_INNER)), _resident((1, D_INNER))],
        out_specs=tok(D_INNER, True),
        out_shape=jax.ShapeDtypeStruct((b, L, D_INNER), BF16),
        scratch_shapes=[state],
        compiler_params=_params("parallel", "arbitrary"),
        name="ssd_bwd",
    )(xc, dt, dt_t, P["a_log_row"], P["a_log_col"], P["head_expand"], y_f, z, P["d_skip_rep"], P["ssm_norm"])


ONES_ROWS = 16
QK_LOOKAHEAD = 3


def _attn_kernel(qt_ref, k_ref, vt_ref, o_ref, m_ref, acc_ref, *, tq, nb):
    kv = pl.program_id(3)

    @pl.when(kv == 0)
    def _():
        m_ref[...] = jnp.full_like(m_ref, -jnp.inf)
        acc_ref[...] = jnp.zeros_like(acc_ref)

    k = k_ref[...]
    vt = vt_ref[...]
    vt_aug = jnp.concatenate([vt, jnp.ones((ONES_ROWS, vt.shape[1]), BF16)], axis=0)
    blocks = [(g, c) for g in range(Q_PER_KV) for c in range(tq // nb)]

    def scores(g, c):
        qt = qt_ref[g * HEAD_DIM:(g + 1) * HEAD_DIM, c * nb:(c + 1) * nb]
        return jnp.dot(k, qt, preferred_element_type=F32)

    pending = [scores(*blk) for blk in blocks[:QK_LOOKAHEAD]]
    for i, (g, c) in enumerate(blocks):
        s = pending.pop(0)
        if i + QK_LOOKAHEAD < len(blocks):
            pending.append(scores(*blocks[i + QK_LOOKAHEAD]))
        cols = slice(g * tq + c * nb, g * tq + (c + 1) * nb)
        m_prev = m_ref[:, cols]
        m_new = jnp.maximum(m_prev, jnp.max(s, axis=0, keepdims=True))
        alpha = jnp.exp2(m_prev - m_new)
        p = jnp.exp2(s - m_new).astype(BF16)
        acc_ref[:, cols] = alpha * acc_ref[:, cols] + jnp.dot(vt_aug, p, preferred_element_type=F32)
        m_ref[:, cols] = m_new

    @pl.when(kv == pl.num_programs(3) - 1)
    def _():
        for g in range(Q_PER_KV):
            acc = acc_ref[:, g * tq:(g + 1) * tq]
            out_t = acc[:HEAD_DIM] / acc[HEAD_DIM:HEAD_DIM + 1]
            o_ref[:, g * HEAD_DIM:(g + 1) * HEAD_DIM] = out_t.T.astype(o_ref.dtype)


def _attn_bounded_kernel(qt_ref, k_ref, vt_ref, o_ref, acc_ref, l_ref, *, tq, nb, kb):
    kv = pl.program_id(3)

    @pl.when(kv == 0)
    def _():
        acc_ref[...] = jnp.zeros_like(acc_ref)
        l_ref[...] = jnp.zeros_like(l_ref)

    tk = k_ref.shape[0]
    blocks = [(t, g, c) for t in range(tk // kb) for g in range(Q_PER_KV) for c in range(tq // nb)]

    def scores(t, g, c):
        qt = qt_ref[g * HEAD_DIM:(g + 1) * HEAD_DIM, c * nb:(c + 1) * nb]
        return jnp.dot(k_ref[t * kb:(t + 1) * kb, :], qt, preferred_element_type=F32)

    pending = [scores(*blk) for blk in blocks[:QK_LOOKAHEAD]]
    for i, (t, g, c) in enumerate(blocks):
        s = pending.pop(0)
        if i + QK_LOOKAHEAD < len(blocks):
            pending.append(scores(*blocks[i + QK_LOOKAHEAD]))
        cols = slice(g * tq + c * nb, g * tq + (c + 1) * nb)
        p = jnp.exp2(s)
        l_ref[:, cols] += jnp.sum(p.reshape(kb // SUBLANES, SUBLANES, nb), axis=0)
        acc_ref[:, cols] += jnp.dot(vt_ref[:, t * kb:(t + 1) * kb], p.astype(BF16), preferred_element_type=F32)

    @pl.when(kv == pl.num_programs(3) - 1)
    def _():
        for g in range(Q_PER_KV):
            cols = slice(g * tq, (g + 1) * tq)
            out_t = acc_ref[:, cols] / jnp.sum(l_ref[:, cols], axis=0, keepdims=True)
            o_ref[:, g * HEAD_DIM:(g + 1) * HEAD_DIM] = out_t.T.astype(o_ref.dtype)


def _attention(qt, k, vt, *, bounded, tq, tk, nb=256):
    b, L, _ = k.shape
    gw = Q_PER_KV * HEAD_DIM
    if bounded:
        body = functools.partial(_attn_bounded_kernel, kb=min(tk, 512))
        scratch = [pltpu.VMEM((HEAD_DIM, Q_PER_KV * tq), F32), pltpu.VMEM((SUBLANES, Q_PER_KV * tq), F32)]
    else:
        body = _attn_kernel
        scratch = [pltpu.VMEM((1, Q_PER_KV * tq), F32), pltpu.VMEM((HEAD_DIM + ONES_ROWS, Q_PER_KV * tq), F32)]
    return pl.pallas_call(
        functools.partial(body, tq=tq, nb=nb),
        grid=(b, N_KV_HEADS, L // tq, L // tk),
        in_specs=[pl.BlockSpec((None, gw, tq), lambda bi, h, i, j: (bi, h, i)),
                  pl.BlockSpec((None, tk, HEAD_DIM), lambda bi, h, i, j: (bi, j, h)),
                  pl.BlockSpec((None, HEAD_DIM, tk), lambda bi, h, i, j: (bi, h, j))],
        out_specs=pl.BlockSpec((None, tq, gw), lambda bi, h, i, j: (bi, i, h)),
        out_shape=jax.ShapeDtypeStruct((b, L, ATTN_WIDTH), BF16),
        scratch_shapes=scratch,
        compiler_params=_params("parallel", "parallel", "parallel", "arbitrary"),
        name="attn_bounded" if bounded else "attn",
    )(qt, k, vt)


SCORE_BOUND = -1.0
_BF16_NORM_SLACK = (1.0 + 2.0 ** -8) ** 2


def _attention_dispatch(qt, k, vt, q_gain, k_gain):
    L = k.shape[1]
    bound = (LOG2_E * HEAD_DIM ** 0.5 * _BF16_NORM_SLACK) * jnp.max(jnp.abs(q_gain)) * jnp.max(jnp.abs(k_gain))
    tq_fast = next(t for t in (2048, 1024, 512) if L % t == 0)
    return lax.cond(bound <= SCORE_BOUND,
                    functools.partial(_attention, bounded=True, tq=tq_fast, tk=min(tq_fast, 1024)),
                    functools.partial(_attention, bounded=False, tq=512, tk=512),
                    qt, k, vt)


def _merge_kernel(s_ref, a_ref, g_ref, h_ref, ws_ref, wa_ref, wo_ref, o_ref):
    ps = jnp.dot(s_ref[...], ws_ref[...], preferred_element_type=F32)
    pa = jnp.dot(a_ref[...], wa_ref[...], preferred_element_type=F32)
    m = g_ref[:, :D_MODEL] * ps + g_ref[:, D_MODEL:] * pa
    o_ref[...] = h_ref[...] + jnp.dot(m.astype(BF16), wo_ref[...], preferred_element_type=F32)


def _merge(s, a, g, h, P, *, tm=512):
    b, L, _ = h.shape

    def tok(width):
        return pl.BlockSpec((None, tm, width), lambda bi, i: (bi, i, 0))

    return pl.pallas_call(
        _merge_kernel,
        grid=(b, L // tm),
        in_specs=[tok(D_INNER), tok(ATTN_WIDTH), tok(2 * D_MODEL), tok(D_MODEL),
                  _resident((D_INNER, D_MODEL)), _resident((ATTN_WIDTH, D_MODEL)),
                  _resident((D_MODEL, D_MODEL))],
        out_specs=tok(D_MODEL),
        out_shape=jax.ShapeDtypeStruct((b, L, D_MODEL), F32),
        compiler_params=_params("parallel", "parallel"),
        name="merge",
    )(s, a, g, h, P["w_ssm_branch"], P["w_attn_branch"], P["w_out"])


def _rope_tables(L):
    rows = L // GRID_W
    row = jnp.repeat(jnp.arange(rows, dtype=F32), GRID_W)
    col = jnp.tile(jnp.arange(GRID_W, dtype=F32), rows)
    axis_dim = HEAD_DIM // 2
    inv_freq = ROPE_THETA ** (-jnp.arange(0, axis_dim, 2, dtype=F32) / axis_dim)
    ang = jnp.concatenate([row[:, None] * inv_freq, col[:, None] * inv_freq], axis=-1)
    cos, sin = jnp.cos(ang), jnp.sin(ang)
    cos2, sin2 = jnp.concatenate([cos, cos], axis=-1), jnp.concatenate([-sin, sin], axis=-1)
    return cos2, sin2, cos2.T, sin2.T


def _layer_params(norm_ffn1, w_ffn1_gu, w_ffn1_down, norm_mix, w_in, conv_w, conv_b, dt_bias_f, dt_bias_b,
                  A_log_f, A_log_b, D_skip, ssm_norm, q_norm, k_norm, w_ssm_branch, w_attn_branch, b_gate,
                  w_out, norm_ffn2, w_ffn2_gu, w_ffn2_down):
    sizes = (D_INNER, CONV_DIM, SSM_HEADS, SSM_HEADS, ATTN_WIDTH, KV_WIDTH, KV_WIDTH, 2 * D_MODEL)
    splits = [int(v) for v in np.cumsum(sizes)[:-1]]
    w_z, w_xbc, w_dtf, w_dtb, w_q, w_k, w_v, w_g = jnp.split(w_in, splits, axis=-1)
    perm = np.concatenate([np.arange(0, HEAD_DIM, 2), np.arange(1, HEAD_DIM, 2)])
    q_cols = np.concatenate([h * HEAD_DIM + perm for h in range(N_Q_HEADS)])
    k_cols = np.concatenate([h * HEAD_DIM + perm for h in range(N_KV_HEADS)])
    w_dt_dup = jnp.concatenate([w_dtf, w_dtf, w_dtb, w_dtb], axis=-1)
    w_dt_pad = jnp.concatenate([w_dtf, w_dtb, jnp.zeros((D_MODEL, DT_PAD - 2 * SSM_HEADS), F32)], axis=-1)
    w_r = jnp.concatenate([w_z, w_xbc, w_k[:, k_cols], w_g, w_dt_dup], axis=-1)
    w_t = jnp.concatenate([w_q[:, q_cols], w_v, w_dt_pad], axis=-1).T
    pad = jnp.zeros((DT_PAD - 2 * SSM_HEADS,), F32)
    head_expand = jnp.repeat(jnp.tile(jnp.eye(SSM_HEADS, dtype=BF16), (2, 1)), SSM_HEAD_DIM, axis=1)
    row = lambda v: v.reshape(1, -1).astype(F32)
    return dict(
        norm_ffn1=row(norm_ffn1), w_ffn1_gu=w_ffn1_gu.astype(BF16), w_ffn1_down=w_ffn1_down.astype(BF16),
        norm_mix=row(norm_mix), w_in=w_r.astype(BF16), w_in_t=w_t.astype(BF16),
        conv_w=jnp.concatenate([conv_w, jnp.zeros((SUBLANES - D_CONV, CONV_DIM), F32)], axis=0),
        conv_b=row(conv_b),
        dt_bias_row=row(jnp.concatenate([dt_bias_f, dt_bias_f, dt_bias_b, dt_bias_b])),
        dt_bias_col=jnp.concatenate([dt_bias_f, dt_bias_b, pad]).reshape(-1, 1),
        a_log_row=row(jnp.concatenate([A_log_f, A_log_f, A_log_b, A_log_b])),
        a_log_col=jnp.concatenate([A_log_f, A_log_b, pad]).reshape(-1, 1), head_expand=head_expand,
        d_skip_rep=row(jnp.repeat(D_skip, SSM_HEAD_DIM)), ssm_norm=row(ssm_norm),
        q_gain_col=q_norm[perm].reshape(-1, 1).astype(F32), k_gain=row(k_norm[perm]),
        w_ssm_branch=w_ssm_branch.astype(BF16), w_attn_branch=w_attn_branch.astype(BF16),
        b_gate=row(b_gate), w_out=w_out.astype(BF16),
        norm_ffn2=row(norm_ffn2), w_ffn2_gu=w_ffn2_gu.astype(BF16), w_ffn2_down=w_ffn2_down.astype(BF16),
    )


def _encoder_layer(x, P, g_next, last):
    L = x.shape[1]
    h, u = _ffn(x, P["norm_ffn1"], P["w_ffn1_gu"], P["w_ffn1_down"], P["norm_mix"],
                emit_h=True, post_dtype=BF16)
    z, xc, q_t, k, v_t, g, dt, dt_t = _inproj(u, P, _rope_tables(L))
    s = _ssd(xc, dt, dt_t, z, P)
    a = _attention_dispatch(q_t, k, v_t, P["q_gain_col"], P["k_gain"])
    h2 = _merge(s, a, g, h, P)
    outs = _ffn(h2, P["norm_ffn2"], P["w_ffn2_gu"], P["w_ffn2_down"], g_next,
                emit_h=not last, post_dtype=F32)
    return outs


def _trunk(x, layers, norm_final):
    depth = len(layers)
    for i, P in enumerate(layers):
        last = i == depth - 1
        g_next = norm_final.reshape(1, -1) if last else layers[i + 1]["norm_ffn1"]
        outs = _encoder_layer(x, P, g_next, last)
        x = outs[0]
    return x


def kernel(x_prompt, x_sample, norm_ffn1, w_ffn1_gu, w_ffn1_down, norm_mix, w_in, conv_w, conv_b, dt_bias_f,
           dt_bias_b, A_log_f, A_log_b, D_skip, ssm_norm, q_norm, k_norm, w_ssm_branch, w_attn_branch, b_gate,
           w_out, norm_ffn2, w_ffn2_gu, w_ffn2_down, norm_final):
    per_layer = (norm_ffn1, w_ffn1_gu, w_ffn1_down, norm_mix, w_in, conv_w, conv_b, dt_bias_f, dt_bias_b,
                 A_log_f, A_log_b, D_skip, ssm_norm, q_norm, k_norm, w_ssm_branch, w_attn_branch, b_gate,
                 w_out, norm_ffn2, w_ffn2_gu, w_ffn2_down)
    depth = norm_ffn1.shape[0]
    layers = [_layer_params(*(w[i] for w in per_layer)) for i in range(depth)]
    return (_trunk(x_prompt, layers, norm_final), _trunk(x_sample, layers, norm_final))
```

```python
import functools

import jax
import jax.numpy as jnp
import numpy as np
from jax import lax
from jax.experimental import pallas as pl
from jax.experimental.pallas import tpu as pltpu

F32 = jnp.float32
BF16 = jnp.bfloat16

D_MODEL = 1024
GRID_W = 64
D_INNER = 2048
SSM_HEAD_DIM = 64
SSM_HEADS = 32
SSM_GROUPS = 4
GROUP_WIDTH = D_INNER // SSM_GROUPS
D_STATE = 128
D_CONV = 5
CHUNK = 128
BC_WIDTH = SSM_GROUPS * D_STATE
CONV_DIM = D_INNER + 2 * BC_WIDTH
HEAD_DIM = 128
N_Q_HEADS = 8
N_KV_HEADS = 2
Q_PER_KV = N_Q_HEADS // N_KV_HEADS
ATTN_WIDTH = N_Q_HEADS * HEAD_DIM
KV_WIDTH = N_KV_HEADS * HEAD_DIM
ROPE_THETA = 10000.0
D_FF = 2816
EPS = 1e-6
DT_PAD = 128

LANES = 128
SUBLANES = 8
VMEM_LIMIT = 56 * 1024 * 1024

LOG2_E = 1.4426950408889634

_OFF_Z = 0
_OFF_XBC = _OFF_Z + D_INNER
_OFF_K = _OFF_XBC + CONV_DIM
_OFF_G = _OFF_K + KV_WIDTH
_OFF_DT = _OFF_G + 2 * D_MODEL
_IN_COLS = _OFF_DT + DT_PAD
_ROW_Q = 0
_ROW_V = _ROW_Q + ATTN_WIDTH
_ROW_DT = _ROW_V + KV_WIDTH
_T_ROWS = _ROW_DT + DT_PAD


def _rms(x, g):
    return x * lax.rsqrt(jnp.mean(x * x, axis=-1, keepdims=True) + EPS) * g


def _silu(x):
    return x * jax.nn.sigmoid(x)


def _softplus(x):
    return jnp.maximum(x, 0.0) + jnp.log1p(jnp.exp(-jnp.abs(x)))


def _params(*sem):
    return pltpu.CompilerParams(dimension_semantics=sem, vmem_limit_bytes=VMEM_LIMIT)


def _resident(shape):
    nd = len(shape)
    return pl.BlockSpec(shape, lambda *_: (0,) * nd, pipeline_mode=pl.Buffered(1))


def _ffn_kernel(x_ref, gpre_ref, wgu_ref, wd_ref, gpost_ref, *out_refs, tf, emit_h):
    x = x_ref[...]
    xn = _rms(x, gpre_ref[...]).astype(BF16)
    acc = jnp.zeros_like(x)
    for c in range(D_FF // tf):
        g = jnp.dot(xn, wgu_ref[:, c * tf:(c + 1) * tf], preferred_element_type=F32)
        u = jnp.dot(xn, wgu_ref[:, D_FF + c * tf:D_FF + (c + 1) * tf], preferred_element_type=F32)
        act = (_silu(g) * u).astype(BF16)
        acc = acc + jnp.dot(act, wd_ref[c * tf:(c + 1) * tf, :], preferred_element_type=F32)
    h = x + 0.5 * acc
    if emit_h:
        out_refs[0][...] = h
    post_ref = out_refs[-1]
    post_ref[...] = _rms(h, gpost_ref[...]).astype(post_ref.dtype)


def _ffn(x, g_pre, w_gu, w_down, g_post, *, emit_h, post_dtype, tm=512, tf=256):
    b, L, _ = x.shape
    tok = pl.BlockSpec((None, tm, D_MODEL), lambda bi, i: (bi, i, 0))
    out_shape = [jax.ShapeDtypeStruct((b, L, D_MODEL), post_dtype)]
    out_specs = [tok]
    if emit_h:
        out_shape.insert(0, jax.ShapeDtypeStruct((b, L, D_MODEL), F32))
        out_specs.insert(0, tok)
    return pl.pallas_call(
        functools.partial(_ffn_kernel, tf=tf, emit_h=emit_h),
        grid=(b, L // tm),
        in_specs=[tok, _resident((1, D_MODEL)), _resident((D_MODEL, 2 * D_FF)),
                  _resident((D_FF, D_MODEL)), _resident((1, D_MODEL))],
        out_specs=out_specs,
        out_shape=out_shape,
        compiler_params=_params("parallel", "parallel"),
        name="ffn",
    )(x, g_pre, w_gu, w_down, g_post)


CONV_CHUNK = 512
U_HALO = 16


def _inproj_kernel(u_ref, uprev_ref, unext_ref, w_ref, wt_ref, cos_ref, sin_ref, cos_t_ref, sin_t_ref, qg_ref,
                   kg_ref, bg_ref, dtb_row_ref, dtb_col_ref, cw_ref, cb_ref,
                   z_ref, xc_ref, qt_ref, k_ref, vt_ref, g_ref, dt_ref, dtt_ref, ext_ref):
    u = u_ref[...]
    tm = u.shape[0]

    def proj(off, width):
        return jnp.dot(u, w_ref[:, off:off + width], preferred_element_type=F32)

    def proj_t(off, rows):
        return lax.dot_general(wt_ref[off:off + rows, :], u, (((1,), (1,)), ((), ())),
                               preferred_element_type=F32)

    i = pl.program_id(1)
    u_ext = jnp.concatenate([uprev_ref[...], u, unext_ref[...]], axis=0)
    first = U_HALO - D_CONV // 2

    def project_chunk(c):
        cols = slice(c * CONV_CHUNK, (c + 1) * CONV_CHUNK)
        x = jnp.dot(u_ext, w_ref[:, _OFF_XBC + c * CONV_CHUNK:_OFF_XBC + (c + 1) * CONV_CHUNK],
                    preferred_element_type=F32)
        ext_ref[0:U_HALO, cols] = jnp.where(i > 0, x[0:U_HALO], 0.0)
        ext_ref[U_HALO:U_HALO + tm, cols] = x[U_HALO:U_HALO + tm]
        ext_ref[U_HALO + tm:, cols] = jnp.where(i < pl.num_programs(1) - 1, x[U_HALO + tm:], 0.0)

    def conv_chunk(c):
        cols = slice(c * CONV_CHUNK, (c + 1) * CONV_CHUNK)
        n_tiles, t0 = tm // SUBLANES, U_HALO // SUBLANES
        xe = ext_ref[:, cols].reshape((tm + 2 * U_HALO) // SUBLANES, SUBLANES, CONV_CHUNK)
        sub = lax.broadcasted_iota(jnp.int32, (1, SUBLANES, CONV_CHUNK), 1)
        acc = jnp.zeros((n_tiles, SUBLANES, CONV_CHUNK), F32) + cb_ref[:, cols]
        for j in range(D_CONV):
            d = j - D_CONV // 2
            if d == 0:
                xs = xe[t0:t0 + n_tiles]
            else:
                rot = pltpu.roll(xe, (SUBLANES - d) % SUBLANES, axis=1)
                if d > 0:
                    xs = jnp.where(sub < SUBLANES - d, rot[t0:t0 + n_tiles], rot[t0 + 1:t0 + n_tiles + 1])
                else:
                    xs = jnp.where(sub < -d, rot[t0 - 1:t0 + n_tiles - 1], rot[t0:t0 + n_tiles])
            acc = acc + xs * cw_ref[j:j + 1, cols]
        xc_ref[:, cols] = _silu(acc).reshape(tm, CONV_CHUNK)

    n_chunks = CONV_DIM // CONV_CHUNK
    project_chunk(0)
    for c in range(n_chunks):
        if c + 1 < n_chunks:
            project_chunk(c + 1)
        conv_chunk(c)
    z_ref[...] = proj(_OFF_Z, D_INNER)
    g_ref[...] = jax.nn.sigmoid(proj(_OFF_G, 2 * D_MODEL) + bg_ref[...])
    dt_ref[...] = _softplus(proj(_OFF_DT, DT_PAD) + dtb_row_ref[...])
    dtt_ref[...] = _softplus(proj_t(_ROW_DT, DT_PAD) + dtb_col_ref[...])
    vt_ref[...] = proj_t(_ROW_V, KV_WIDTH).astype(BF16)

    half = HEAD_DIM // 2
    cos = cos_ref[...]
    sin = sin_ref[...]
    for h in range(N_KV_HEADS):
        x = _rms(proj(_OFF_K + h * HEAD_DIM, HEAD_DIM), kg_ref[...])
        k_ref[:, h * HEAD_DIM:(h + 1) * HEAD_DIM] = (x * cos + pltpu.roll(x, half, axis=1) * sin).astype(BF16)

    cos_t = cos_t_ref[...]
    sin_t = sin_t_ref[...]
    q_scale = HEAD_DIM ** -0.5 * LOG2_E
    for h in range(N_Q_HEADS):
        x = proj_t(_ROW_Q + h * HEAD_DIM, HEAD_DIM)
        x = x * lax.rsqrt(jnp.mean(x * x, axis=0, keepdims=True) + EPS) * qg_ref[...]
        rot = jnp.concatenate([x[half:], x[:half]], axis=0)
        qt_ref[h * HEAD_DIM:(h + 1) * HEAD_DIM, :] = ((x * cos_t + rot * sin_t) * q_scale).astype(BF16)


def _inproj(u, P, tables, *, tm=256):
    b, L, _ = u.shape

    def tok(width):
        return pl.BlockSpec((None, tm, width), lambda bi, i: (bi, i, 0))

    def tok_t(rows):
        return pl.BlockSpec((None, rows, tm), lambda bi, i: (bi, 0, i))

    pos = pl.BlockSpec((tm, HEAD_DIM), lambda bi, i: (i, 0))
    pos_t = pl.BlockSpec((HEAD_DIM, tm), lambda bi, i: (0, i))
    outs = [(tok, D_INNER, F32), (tok, CONV_DIM, F32), (tok_t, ATTN_WIDTH, BF16), (tok, KV_WIDTH, BF16),
            (tok_t, KV_WIDTH, BF16), (tok, 2 * D_MODEL, F32), (tok, DT_PAD, F32), (tok_t, DT_PAD, F32)]
    out_shape = [jax.ShapeDtypeStruct((b, L, w) if mk is tok else (b, w, L), dt) for mk, w, dt in outs]
    out_specs = [mk(w) for mk, w, _ in outs]
    cos, sin, cos_t, sin_t = tables
    r = tm // U_HALO
    last = L // U_HALO - 1
    halo_prev = pl.BlockSpec((None, U_HALO, D_MODEL), lambda bi, i: (bi, jnp.maximum(i * r - 1, 0), 0))
    halo_next = pl.BlockSpec((None, U_HALO, D_MODEL), lambda bi, i: (bi, jnp.minimum((i + 1) * r, last), 0))
    return pl.pallas_call(
        _inproj_kernel,
        grid=(b, L // tm),
        in_specs=[tok(D_MODEL), halo_prev, halo_next, _resident((D_MODEL, _IN_COLS)),
                  _resident((_T_ROWS, D_MODEL)), pos, pos,
                  pos_t, pos_t, _resident((HEAD_DIM, 1)), _resident((1, HEAD_DIM)), _resident((1, 2 * D_MODEL)),
                  _resident((1, DT_PAD)), _resident((DT_PAD, 1)),
                  _resident((SUBLANES, CONV_DIM)), _resident((1, CONV_DIM))],
        out_specs=out_specs,
        out_shape=out_shape,
        scratch_shapes=[pltpu.VMEM((tm + 2 * U_HALO, CONV_DIM), F32)],
        compiler_params=_params("parallel", "parallel"),
        name="in_proj",
    )(u, u, u, P["w_in"], P["w_in_t"], cos, sin, cos_t, sin_t, P["q_gain_col"], P["k_gain"], P["b_gate"],
      P["dt_bias_row"], P["dt_bias_col"], P["conv_w"], P["conv_b"])


DUP = 2 * SSM_HEADS


def _chunk_masks(rev):
    row_i = lax.broadcasted_iota(jnp.int32, (CHUNK, CHUNK), 0)
    col_i = lax.broadcasted_iota(jnp.int32, (CHUNK, CHUNK), 1)
    tri = (col_i >= row_i) if rev else (col_i <= row_i)
    return tri, tri.astype(F32).astype(BF16)


def _split3(x):
    hi = x.astype(BF16)
    r = x - hi.astype(F32)
    mid = r.astype(BF16)
    lo = (r - mid.astype(F32)).astype(BF16)
    return hi, mid, lo


def _ssd_prepare(dt2, dt_t, a2_row, a_col, bm, cm, expand_ref, tri_b, rev):
    tri3 = jnp.concatenate([tri_b, tri_b, tri_b], axis=1)
    acum2 = jnp.dot(tri3, jnp.concatenate(_split3(dt2 * a2_row), axis=0),
                    preferred_element_type=F32)
    acum_t = lax.dot_general(jnp.concatenate(_split3(dt_t * a_col), axis=1), tri3, (((1,), (1,)), ((), ())),
                             preferred_element_type=F32)
    end = 0 if rev else CHUNK - 1
    total2 = acum2[end:end + 1, :]
    ew = jnp.concatenate([jnp.exp2(acum2), jnp.exp2(total2 - acum2) * dt2], axis=0)
    ew_hi = ew.astype(BF16)
    ew_lo = (ew - ew_hi.astype(F32)).astype(BF16)
    lane = lax.broadcasted_iota(jnp.int32, ew.shape, 1)
    rep = jnp.dot(jnp.where(lane < SSM_HEADS, ew_hi, ew_lo), expand_ref[...],
                  preferred_element_type=F32)
    r_t = acum_t - jnp.log2(dt_t)
    b_gs = [bm[:, g * D_STATE:(g + 1) * D_STATE].astype(BF16) for g in range(SSM_GROUPS)]
    c_gs = [cm[:, g * D_STATE:(g + 1) * D_STATE].astype(BF16) for g in range(SSM_GROUPS)]
    cbs = []
    zero = jnp.zeros((CHUNK, D_STATE), BF16)
    for g in range(0, SSM_GROUPS, 2):
        b_diag = jnp.concatenate([jnp.concatenate([b_gs[g], zero], axis=1),
                                  jnp.concatenate([zero, b_gs[g + 1]], axis=1)], axis=0)
        cb2 = lax.dot_general(jnp.concatenate([c_gs[g], c_gs[g + 1]], axis=1), b_diag, (((1,), (1,)), ((), ())),
                              preferred_element_type=F32)
        cbs += [cb2[:, :CHUNK], cb2[:, CHUNK:]]
    return acum2, r_t, rep[:CHUNK], rep[CHUNK:], b_gs, c_gs, cbs


def _ssd_finish(prep, xs, state_ref, tri, rev):
    acum2, r_t, e_rep, w_rep, b_gs, c_gs, cbs = prep
    end = 0 if rev else CHUNK - 1
    cd_rep = e_rep[end:end + 1, :]
    y_offs = []
    for g in range(SSM_GROUPS):
        cols = slice(g * GROUP_WIDTH, (g + 1) * GROUP_WIDTH)
        prev = state_ref[g]
        y_offs.append(jnp.dot(c_gs[g], prev.astype(BF16), preferred_element_type=F32) * e_rep[:, cols])
        s_new = lax.dot_general(b_gs[g], (xs[:, cols] * w_rep[:, cols]).astype(BF16), (((0,), (0,)), ((), ())),
                                preferred_element_type=F32)
        state_ref[g] = prev * cd_rep[:, cols] + s_new
    lane = lax.broadcasted_iota(jnp.int32, (CHUNK, LANES), 1)
    lo_half = lane < SSM_HEAD_DIM
    ys = []
    for g in range(SSM_GROUPS):
        pairs = []
        for j in range(GROUP_WIDTH // LANES):
            ms = []
            for hh in range(2):
                h = g * (SSM_HEADS // SSM_GROUPS) + 2 * j + hh
                seg = acum2[:, h:h + 1] - r_t[h:h + 1, :]
                ms.append(cbs[g] * jnp.exp2(jnp.where(tri, seg, -jnp.inf)))
            lhs = jnp.concatenate(ms, axis=1).astype(BF16)
            xp = xs[:, g * GROUP_WIDTH + j * LANES:g * GROUP_WIDTH + (j + 1) * LANES]
            rhs = jnp.concatenate([jnp.where(lo_half, xp, 0.0), jnp.where(lo_half, 0.0, xp)],
                                  axis=0).astype(BF16)
            pairs.append(jnp.dot(lhs, rhs, preferred_element_type=F32))
        ys.append(jnp.concatenate(pairs, axis=1) + y_offs[g])
    return jnp.concatenate(ys, axis=1)


def _ssd_scan(xc_ref, dt_ref, dtt_ref, alog_row_ref, alog_col_ref, expand_ref, state_ref, emit, *, nchunk, rev):
    @pl.when(pl.program_id(1) == 0)
    def _():
        state_ref[...] = jnp.zeros_like(state_ref)

    lo, lo_t = (DUP, SSM_HEADS) if rev else (0, 0)
    a2_row = -jnp.exp(alog_row_ref[:, lo:lo + DUP]) * LOG2_E
    a_col = -jnp.exp(alog_col_ref[lo_t:lo_t + SSM_HEADS, :]) * LOG2_E
    tri, tri_b = _chunk_masks(rev)

    def prepare(c):
        rows = slice(c * CHUNK, (c + 1) * CHUNK)
        return _ssd_prepare(dt_ref[rows, lo:lo + DUP], dtt_ref[lo_t:lo_t + SSM_HEADS, rows], a2_row, a_col,
                            xc_ref[rows, D_INNER:D_INNER + BC_WIDTH], xc_ref[rows, D_INNER + BC_WIDTH:],
                            expand_ref, tri_b, rev)

    order = list(range(nchunk))[::-1] if rev else list(range(nchunk))
    nxt = prepare(order[0])
    for i, c in enumerate(order):
        cur = nxt
        if i + 1 < nchunk:
            nxt = prepare(order[i + 1])
        rows = slice(c * CHUNK, (c + 1) * CHUNK)
        xs = xc_ref[rows, :D_INNER]
        emit(rows, xs, _ssd_finish(cur, xs, state_ref, tri, rev))


def _ssd_fwd_kernel(xc_ref, dt_ref, dtt_ref, alog_row_ref, alog_col_ref, expand_ref, y_ref, state_ref, *, nchunk):
    def emit(rows, xs, y):
        y_ref[rows, :] = y

    _ssd_scan(xc_ref, dt_ref, dtt_ref, alog_row_ref, alog_col_ref, expand_ref, state_ref, emit,
              nchunk=nchunk, rev=False)


def _ssd_bwd_kernel(xc_ref, dt_ref, dtt_ref, alog_row_ref, alog_col_ref, expand_ref, yf_ref, z_ref, dskip_ref,
                    gain_ref, s_ref, state_ref, *, nchunk):
    def emit(rows, xs, y_b):
        y = (yf_ref[rows, :] + y_b + dskip_ref[...] * xs) * _silu(z_ref[rows, :])
        for g in range(SSM_GROUPS):
            cols = slice(g * GROUP_WIDTH, (g + 1) * GROUP_WIDTH)
            s_ref[rows, cols] = _rms(y[:, cols], gain_ref[:, cols]).astype(s_ref.dtype)

    _ssd_scan(xc_ref, dt_ref, dtt_ref, alog_row_ref, alog_col_ref, expand_ref, state_ref, emit,
              nchunk=nchunk, rev=True)


def _ssd(xc, dt, dt_t, z, P, *, nchunk=4):
    b, L, _ = xc.shape
    tl = nchunk * CHUNK
    nb = L // tl

    def tok(width, rev):
        if rev:
            return pl.BlockSpec((None, tl, width), lambda bi, c: (bi, nb - 1 - c, 0))
        return pl.BlockSpec((None, tl, width), lambda bi, c: (bi, c, 0))

    def tok_t(rev):
        if rev:
            return pl.BlockSpec((None, DT_PAD, tl), lambda bi, c: (bi, 0, nb - 1 - c))
        return pl.BlockSpec((None, DT_PAD, tl), lambda bi, c: (bi, 0, c))

    state = pltpu.VMEM((SSM_GROUPS, D_STATE, GROUP_WIDTH), F32)
    consts = [_resident((1, DT_PAD)), _resident((DT_PAD, 1)), _resident((DUP, D_INNER))]
    y_f = pl.pallas_call(
        functools.partial(_ssd_fwd_kernel, nchunk=nchunk),
        grid=(b, nb),
        in_specs=[tok(CONV_DIM, False), tok(DT_PAD, False), tok_t(False)] + consts,
        out_specs=tok(D_INNER, False),
        out_shape=jax.ShapeDtypeStruct((b, L, D_INNER), F32),
        scratch_shapes=[state],
        compiler_params=_params("parallel", "arbitrary"),
        name="ssd_fwd",
    )(xc, dt, dt_t, P["a_log_row"], P["a_log_col"], P["head_expand"])
    return pl.pallas_call(
        functools.partial(_ssd_bwd_kernel, nchunk=nchunk),
        grid=(b, nb),
        in_specs=[tok(CONV_DIM, True), tok(DT_PAD, True), tok_t(True)] + consts
                 + [tok(D_INNER, True), tok(D_INNER, True), _resident((1, D_INNER)), _resident((1, D_INNER))],
        out_specs=tok(D_INNER, True),
        out_shape=jax.ShapeDtypeStruct((b, L, D_INNER), BF16),
        scratch_shapes=[state],
        compiler_params=_params("parallel", "arbitrary"),
        name="ssd_bwd",
    )(xc, dt, dt_t, P["a_log_row"], P["a_log_col"], P["head_expand"], y_f, z, P["d_skip_rep"], P["ssm_norm"])


ONES_ROWS = 16
QK_LOOKAHEAD = 3


def _attn_kernel(qt_ref, k_ref, vt_ref, o_ref, m_ref, acc_ref, *, tq, nb):
    kv = pl.program_id(3)

    @pl.when(kv == 0)
    def _():
        m_ref[...] = jnp.full_like(m_ref, -jnp.inf)
        acc_ref[...] = jnp.zeros_like(acc_ref)

    k = k_ref[...]
    vt = vt_ref[...]
    vt_aug = jnp.concatenate([vt, jnp.ones((ONES_ROWS, vt.shape[1]), BF16)], axis=0)
    blocks = [(g, c) for g in range(Q_PER_KV) for c in range(tq // nb)]

    def scores(g, c):
        qt = qt_ref[g * HEAD_DIM:(g + 1) * HEAD_DIM, c * nb:(c + 1) * nb]
        return jnp.dot(k, qt, preferred_element_type=F32)

    pending = [scores(*blk) for blk in blocks[:QK_LOOKAHEAD]]
    for i, (g, c) in enumerate(blocks):
        s = pending.pop(0)
        if i + QK_LOOKAHEAD < len(blocks):
            pending.append(scores(*blocks[i + QK_LOOKAHEAD]))
        cols = slice(g * tq + c * nb, g * tq + (c + 1) * nb)
        m_prev = m_ref[:, cols]
        m_new = jnp.maximum(m_prev, jnp.max(s, axis=0, keepdims=True))
        alpha = jnp.exp2(m_prev - m_new)
        p = jnp.exp2(s - m_new).astype(BF16)
        acc_ref[:, cols] = alpha * acc_ref[:, cols] + jnp.dot(vt_aug, p, preferred_element_type=F32)
        m_ref[:, cols] = m_new

    @pl.when(kv == pl.num_programs(3) - 1)
    def _():
        for g in range(Q_PER_KV):
            acc = acc_ref[:, g * tq:(g + 1) * tq]
            out_t = acc[:HEAD_DIM] / acc[HEAD_DIM:HEAD_DIM + 1]
            o_ref[:, g * HEAD_DIM:(g + 1) * HEAD_DIM] = out_t.T.astype(o_ref.dtype)


def _attn_bounded_kernel(qt_ref, k_ref, vt_ref, o_ref, acc_ref, l_ref, *, tq, nb, kb):
    kv = pl.program_id(3)

    @pl.when(kv == 0)
    def _():
        acc_ref[...] = jnp.zeros_like(acc_ref)
        l_ref[...] = jnp.zeros_like(l_ref)

    tk = k_ref.shape[0]
    blocks = [(t, g, c) for t in range(tk // kb) for g in range(Q_PER_KV) for c in range(tq // nb)]

    def scores(t, g, c):
        qt = qt_ref[g * HEAD_DIM:(g + 1) * HEAD_DIM, c * nb:(c + 1) * nb]
        return jnp.dot(k_ref[t * kb:(t + 1) * kb, :], qt, preferred_element_type=F32)

    pending = [scores(*blk) for blk in blocks[:QK_LOOKAHEAD]]
    for i, (t, g, c) in enumerate(blocks):
        s = pending.pop(0)
        if i + QK_LOOKAHEAD < len(blocks):
            pending.append(scores(*blocks[i + QK_LOOKAHEAD]))
        cols = slice(g * tq + c * nb, g * tq + (c + 1) * nb)
        p = jnp.exp2(s)
        l_ref[:, cols] += jnp.sum(p.reshape(kb // SUBLANES, SUBLANES, nb), axis=0)
        acc_ref[:, cols] += jnp.dot(vt_ref[:, t * kb:(t + 1) * kb], p.astype(BF16), preferred_element_type=F32)

    @pl.when(kv == pl.num_programs(3) - 1)
    def _():
        for g in range(Q_PER_KV):
            cols = slice(g * tq, (g + 1) * tq)
            out_t = acc_ref[:, cols] / jnp.sum(l_ref[:, cols], axis=0, keepdims=True)
            o_ref[:, g * HEAD_DIM:(g + 1) * HEAD_DIM] = out_t.T.astype(o_ref.dtype)


def _attention(qt, k, vt, *, bounded, tq, tk, nb=256):
    b, L, _ = k.shape
    gw = Q_PER_KV * HEAD_DIM
    if bounded:
        body = functools.partial(_attn_bounded_kernel, kb=min(tk, 512))
        scratch = [pltpu.VMEM((HEAD_DIM, Q_PER_KV * tq), F32), pltpu.VMEM((SUBLANES, Q_PER_KV * tq), F32)]
    else:
        body = _attn_kernel
        scratch = [pltpu.VMEM((1, Q_PER_KV * tq), F32), pltpu.VMEM((HEAD_DIM + ONES_ROWS, Q_PER_KV * tq), F32)]
    return pl.pallas_call(
        functools.partial(body, tq=tq, nb=nb),
        grid=(b, N_KV_HEADS, L // tq, L // tk),
        in_specs=[pl.BlockSpec((None, gw, tq), lambda bi, h, i, j: (bi, h, i)),
                  pl.BlockSpec((None, tk, HEAD_DIM), lambda bi, h, i, j: (bi, j, h)),
                  pl.BlockSpec((None, HEAD_DIM, tk), lambda bi, h, i, j: (bi, h, j))],
        out_specs=pl.BlockSpec((None, tq, gw), lambda bi, h, i, j: (bi, i, h)),
        out_shape=jax.ShapeDtypeStruct((b, L, ATTN_WIDTH), BF16),
        scratch_shapes=scratch,
        compiler_params=_params("parallel", "parallel", "parallel", "arbitrary"),
        name="attn_bounded" if bounded else "attn",
    )(qt, k, vt)


SCORE_BOUND = 40.0
_BF16_NORM_SLACK = (1.0 + 2.0 ** -8) ** 2


def _attention_dispatch(qt, k, vt, q_gain, k_gain):
    L = k.shape[1]
    bound = (LOG2_E * HEAD_DIM ** 0.5 * _BF16_NORM_SLACK) * jnp.max(jnp.abs(q_gain)) * jnp.max(jnp.abs(k_gain))
    tq_fast = next(t for t in (2048, 1024, 512) if L % t == 0)
    return lax.cond(bound <= SCORE_BOUND,
                    functools.partial(_attention, bounded=True, tq=tq_fast, tk=tq_fast),
                    functools.partial(_attention, bounded=False, tq=512, tk=512),
                    qt, k, vt)


def _merge_kernel(s_ref, a_ref, g_ref, h_ref, ws_ref, wa_ref, wo_ref, o_ref):
    ps = jnp.dot(s_ref[...], ws_ref[...], preferred_element_type=F32)
    pa = jnp.dot(a_ref[...], wa_ref[...], preferred_element_type=F32)
    m = g_ref[:, :D_MODEL] * ps + g_ref[:, D_MODEL:] * pa
    o_ref[...] = h_ref[...] + jnp.dot(m.astype(BF16), wo_ref[...], preferred_element_type=F32)


def _merge(s, a, g, h, P, *, tm=512):
    b, L, _ = h.shape

    def tok(width):
        return pl.BlockSpec((None, tm, width), lambda bi, i: (bi, i, 0))

    return pl.pallas_call(
        _merge_kernel,
        grid=(b, L // tm),
        in_specs=[tok(D_INNER), tok(ATTN_WIDTH), tok(2 * D_MODEL), tok(D_MODEL),
                  _resident((D_INNER, D_MODEL)), _resident((ATTN_WIDTH, D_MODEL)),
                  _resident((D_MODEL, D_MODEL))],
        out_specs=tok(D_MODEL),
        out_shape=jax.ShapeDtypeStruct((b, L, D_MODEL), F32),
        compiler_params=_params("parallel", "parallel"),
        name="merge",
    )(s, a, g, h, P["w_ssm_branch"], P["w_attn_branch"], P["w_out"])


def _rope_tables(L):
    rows = L // GRID_W
    row = jnp.repeat(jnp.arange(rows, dtype=F32), GRID_W)
    col = jnp.tile(jnp.arange(GRID_W, dtype=F32), rows)
    axis_dim = HEAD_DIM // 2
    inv_freq = ROPE_THETA ** (-jnp.arange(0, axis_dim, 2, dtype=F32) / axis_dim)
    ang = jnp.concatenate([row[:, None] * inv_freq, col[:, None] * inv_freq], axis=-1)
    cos, sin = jnp.cos(ang), jnp.sin(ang)
    cos2, sin2 = jnp.concatenate([cos, cos], axis=-1), jnp.concatenate([-sin, sin], axis=-1)
    return cos2, sin2, cos2.T, sin2.T


def _layer_params(norm_ffn1, w_ffn1_gu, w_ffn1_down, norm_mix, w_in, conv_w, conv_b, dt_bias_f, dt_bias_b,
                  A_log_f, A_log_b, D_skip, ssm_norm, q_norm, k_norm, w_ssm_branch, w_attn_branch, b_gate,
                  w_out, norm_ffn2, w_ffn2_gu, w_ffn2_down):
    sizes = (D_INNER, CONV_DIM, SSM_HEADS, SSM_HEADS, ATTN_WIDTH, KV_WIDTH, KV_WIDTH, 2 * D_MODEL)
    splits = [int(v) for v in np.cumsum(sizes)[:-1]]
    w_z, w_xbc, w_dtf, w_dtb, w_q, w_k, w_v, w_g = jnp.split(w_in, splits, axis=-1)
    perm = np.concatenate([np.arange(0, HEAD_DIM, 2), np.arange(1, HEAD_DIM, 2)])
    q_cols = np.concatenate([h * HEAD_DIM + perm for h in range(N_Q_HEADS)])
    k_cols = np.concatenate([h * HEAD_DIM + perm for h in range(N_KV_HEADS)])
    w_dt_dup = jnp.concatenate([w_dtf, w_dtf, w_dtb, w_dtb], axis=-1)
    w_dt_pad = jnp.concatenate([w_dtf, w_dtb, jnp.zeros((D_MODEL, DT_PAD - 2 * SSM_HEADS), F32)], axis=-1)
    w_r = jnp.concatenate([w_z, w_xbc, w_k[:, k_cols], w_g, w_dt_dup], axis=-1)
    w_t = jnp.concatenate([w_q[:, q_cols], w_v, w_dt_pad], axis=-1).T
    pad = jnp.zeros((DT_PAD - 2 * SSM_HEADS,), F32)
    head_expand = jnp.repeat(jnp.tile(jnp.eye(SSM_HEADS, dtype=BF16), (2, 1)), SSM_HEAD_DIM, axis=1)
    row = lambda v: v.reshape(1, -1).astype(F32)
    return dict(
        norm_ffn1=row(norm_ffn1), w_ffn1_gu=w_ffn1_gu.astype(BF16), w_ffn1_down=w_ffn1_down.astype(BF16),
        norm_mix=row(norm_mix), w_in=w_r.astype(BF16), w_in_t=w_t.astype(BF16),
        conv_w=jnp.concatenate([conv_w, jnp.zeros((SUBLANES - D_CONV, CONV_DIM), F32)], axis=0),
        conv_b=row(conv_b),
        dt_bias_row=row(jnp.concatenate([dt_bias_f, dt_bias_f, dt_bias_b, dt_bias_b])),
        dt_bias_col=jnp.concatenate([dt_bias_f, dt_bias_b, pad]).reshape(-1, 1),
        a_log_row=row(jnp.concatenate([A_log_f, A_log_f, A_log_b, A_log_b])),
        a_log_col=jnp.concatenate([A_log_f, A_log_b, pad]).reshape(-1, 1), head_expand=head_expand,
        d_skip_rep=row(jnp.repeat(D_skip, SSM_HEAD_DIM)), ssm_norm=row(ssm_norm),
        q_gain_col=q_norm[perm].reshape(-1, 1).astype(F32), k_gain=row(k_norm[perm]),
        w_ssm_branch=w_ssm_branch.astype(BF16), w_attn_branch=w_attn_branch.astype(BF16),
        b_gate=row(b_gate), w_out=w_out.astype(BF16),
        norm_ffn2=row(norm_ffn2), w_ffn2_gu=w_ffn2_gu.astype(BF16), w_ffn2_down=w_ffn2_down.astype(BF16),
    )


def _encoder_layer(x, P, g_next, last):
    L = x.shape[1]
    h, u = _ffn(x, P["norm_ffn1"], P["w_ffn1_gu"], P["w_ffn1_down"], P["norm_mix"],
                emit_h=True, post_dtype=BF16)
    z, xc, q_t, k, v_t, g, dt, dt_t = _inproj(u, P, _rope_tables(L))
    s = _ssd(xc, dt, dt_t, z, P)
    a = _attention_dispatch(q_t, k, v_t, P["q_gain_col"], P["k_gain"])
    h2 = _merge(s, a, g, h, P)
    outs = _ffn(h2, P["norm_ffn2"], P["w_ffn2_gu"], P["w_ffn2_down"], g_next,
                emit_h=not last, post_dtype=F32)
    return outs


def _trunk(x, layers, norm_final):
    depth = len(layers)
    for i, P in enumerate(layers):
        last = i == depth - 1
        g_next = norm_final.reshape(1, -1) if last else layers[i + 1]["norm_ffn1"]
        outs = _encoder_layer(x, P, g_next, last)
        x = outs[0]
    return x


def kernel(x_prompt, x_sample, norm_ffn1, w_ffn1_gu, w_ffn1_down, norm_mix, w_in, conv_w, conv_b, dt_bias_f,
           dt_bias_b, A_log_f, A_log_b, D_skip, ssm_norm, q_norm, k_norm, w_ssm_branch, w_attn_branch, b_gate,
           w_out, norm_ffn2, w_ffn2_gu, w_ffn2_down, norm_final):
    per_layer = (norm_ffn1, w_ffn1_gu, w_ffn1_down, norm_mix, w_in, conv_w, conv_b, dt_bias_f, dt_bias_b,
                 A_log_f, A_log_b, D_skip, ssm_norm, q_norm, k_norm, w_ssm_branch, w_attn_branch, b_gate,
                 w_out, norm_ffn2, w_ffn2_gu, w_ffn2_down)
    depth = norm_ffn1.shape[0]
    layers = [_layer_params(*(w[i] for w in per_layer)) for i in range(depth)]
    return (_trunk(x_prompt, layers, norm_final), _trunk(x_sample, layers, norm_final))
```

```python
import functools

import jax
import jax.numpy as jnp
import numpy as np
from jax import lax
from jax.experimental import pallas as pl
from jax.experimental.pallas import tpu as pltpu

F32 = jnp.float32
BF16 = jnp.bfloat16

D_MODEL = 1024
GRID_W = 64
D_INNER = 2048
SSM_HEAD_DIM = 64
SSM_HEADS = 32
SSM_GROUPS = 4
GROUP_WIDTH = D_INNER // SSM_GROUPS
D_STATE = 128
D_CONV = 5
CHUNK = 128
BC_WIDTH = SSM_GROUPS * D_STATE
CONV_DIM = D_INNER + 2 * BC_WIDTH
HEAD_DIM = 128
N_Q_HEADS = 8
N_KV_HEADS = 2
Q_PER_KV = N_Q_HEADS // N_KV_HEADS
ATTN_WIDTH = N_Q_HEADS * HEAD_DIM
KV_WIDTH = N_KV_HEADS * HEAD_DIM
ROPE_THETA = 10000.0
D_FF = 2816
EPS = 1e-6
DT_PAD = 128

LANES = 128
SUBLANES = 8
VMEM_LIMIT = 56 * 1024 * 1024

LOG2_E = 1.4426950408889634

_OFF_Z = 0
_OFF_XBC = _OFF_Z + D_INNER
_OFF_K = _OFF_XBC + CONV_DIM
_OFF_G = _OFF_K + KV_WIDTH
_OFF_DT = _OFF_G + 2 * D_MODEL
_IN_COLS = _OFF_DT + DT_PAD
_ROW_Q = 0
_ROW_V = _ROW_Q + ATTN_WIDTH
_ROW_DT = _ROW_V + KV_WIDTH
_T_ROWS = _ROW_DT + DT_PAD


def _rms(x, g):
    return x * lax.rsqrt(jnp.mean(x * x, axis=-1, keepdims=True) + EPS) * g


def _silu(x):
    return x * jax.nn.sigmoid(x)


def _softplus(x):
    return jnp.maximum(x, 0.0) + jnp.log1p(jnp.exp(-jnp.abs(x)))


def _params(*sem):
    return pltpu.CompilerParams(dimension_semantics=sem, vmem_limit_bytes=VMEM_LIMIT)


def _resident(shape):
    nd = len(shape)
    return pl.BlockSpec(shape, lambda *_: (0,) * nd, pipeline_mode=pl.Buffered(1))


def _ffn_body(x, gpre_ref, wgu_ref, wd_ref, gpost_ref, out_refs, tf, emit_h):
    xn = _rms(x, gpre_ref[...]).astype(BF16)
    acc = jnp.zeros_like(x)
    for c in range(D_FF // tf):
        g = jnp.dot(xn, wgu_ref[:, c * tf:(c + 1) * tf], preferred_element_type=F32)
        u = jnp.dot(xn, wgu_ref[:, D_FF + c * tf:D_FF + (c + 1) * tf], preferred_element_type=F32)
        act = (_silu(g) * u).astype(BF16)
        acc = acc + jnp.dot(act, wd_ref[c * tf:(c + 1) * tf, :], preferred_element_type=F32)
    h = x + 0.5 * acc
    if emit_h:
        out_refs[0][...] = h
    post_ref = out_refs[-1]
    post_ref[...] = _rms(h, gpost_ref[...]).astype(post_ref.dtype)


def _ffn_kernel(x_ref, gpre_ref, wgu_ref, wd_ref, gpost_ref, *out_refs, tf, emit_h):
    _ffn_body(x_ref[...], gpre_ref, wgu_ref, wd_ref, gpost_ref, out_refs, tf, emit_h)


def _merge_ffn_kernel(s_ref, a_ref, g_ref, h_ref, ws_ref, wa_ref, wo_ref, gpre_ref, wgu_ref, wd_ref, gpost_ref,
                      *out_refs, tf, emit_h):
    ps = jnp.dot(s_ref[...], ws_ref[...], preferred_element_type=F32)
    pa = jnp.dot(a_ref[...], wa_ref[...], preferred_element_type=F32)
    m = g_ref[:, :D_MODEL] * ps + g_ref[:, D_MODEL:] * pa
    x = h_ref[...] + jnp.dot(m.astype(BF16), wo_ref[...], preferred_element_type=F32)
    _ffn_body(x, gpre_ref, wgu_ref, wd_ref, gpost_ref, out_refs, tf, emit_h)


def _ffn_call(body, token_inputs, weights, b, L, *, emit_h, post_dtype, tm):
    def tok(width):
        return pl.BlockSpec((None, tm, width), lambda bi, i: (bi, i, 0))

    out_shape = [jax.ShapeDtypeStruct((b, L, D_MODEL), post_dtype)]
    out_specs = [tok(D_MODEL)]
    if emit_h:
        out_shape.insert(0, jax.ShapeDtypeStruct((b, L, D_MODEL), F32))
        out_specs.insert(0, tok(D_MODEL))
    return pl.pallas_call(
        body,
        grid=(b, L // tm),
        in_specs=[tok(t.shape[-1]) for t in token_inputs] + [_resident(w.shape) for w in weights],
        out_specs=out_specs,
        out_shape=out_shape,
        compiler_params=_params("parallel", "parallel"),
        name="ffn" if len(token_inputs) == 1 else "merge_ffn",
    )(*token_inputs, *weights)


def _ffn(x, g_pre, w_gu, w_down, g_post, *, emit_h, post_dtype, tm=512, tf=256):
    b, L, _ = x.shape
    return _ffn_call(functools.partial(_ffn_kernel, tf=tf, emit_h=emit_h), [x], [g_pre, w_gu, w_down, g_post],
                     b, L, emit_h=emit_h, post_dtype=post_dtype, tm=tm)


def _merge_ffn(s, a, g, h, P, g_post, *, emit_h, post_dtype, tm=512, tf=256):
    b, L, _ = h.shape
    weights = [P["w_ssm_branch"], P["w_attn_branch"], P["w_out"], P["norm_ffn2"], P["w_ffn2_gu"],
               P["w_ffn2_down"], g_post]
    return _ffn_call(functools.partial(_merge_ffn_kernel, tf=tf, emit_h=emit_h), [s, a, g, h], weights,
                     b, L, emit_h=emit_h, post_dtype=post_dtype, tm=tm)


CONV_CHUNK = 512
U_HALO = 16


def _inproj_kernel(u_ref, uprev_ref, unext_ref, w_ref, wt_ref, cos_ref, sin_ref, cos_t_ref, sin_t_ref, qg_ref,
                   kg_ref, bg_ref, dtb_row_ref, dtb_col_ref, cw_ref, cb_ref,
                   z_ref, xc_ref, qt_ref, k_ref, vt_ref, g_ref, dt_ref, dtt_ref, ext_ref):
    u = u_ref[...]
    tm = u.shape[0]

    def proj(off, width):
        return jnp.dot(u, w_ref[:, off:off + width], preferred_element_type=F32)

    def proj_t(off, rows):
        return lax.dot_general(wt_ref[off:off + rows, :], u, (((1,), (1,)), ((), ())),
                               preferred_element_type=F32)

    i = pl.program_id(1)
    u_ext = jnp.concatenate([uprev_ref[...], u, unext_ref[...]], axis=0)
    first = U_HALO - D_CONV // 2

    def project_chunk(c):
        cols = slice(c * CONV_CHUNK, (c + 1) * CONV_CHUNK)
        x = jnp.dot(u_ext, w_ref[:, _OFF_XBC + c * CONV_CHUNK:_OFF_XBC + (c + 1) * CONV_CHUNK],
                    preferred_element_type=F32)
        ext_ref[0:U_HALO, cols] = jnp.where(i > 0, x[0:U_HALO], 0.0)
        ext_ref[U_HALO:U_HALO + tm, cols] = x[U_HALO:U_HALO + tm]
        ext_ref[U_HALO + tm:, cols] = jnp.where(i < pl.num_programs(1) - 1, x[U_HALO + tm:], 0.0)

    def conv_chunk(c):
        cols = slice(c * CONV_CHUNK, (c + 1) * CONV_CHUNK)
        n_tiles, t0 = tm // SUBLANES, U_HALO // SUBLANES
        xe = ext_ref[:, cols].reshape((tm + 2 * U_HALO) // SUBLANES, SUBLANES, CONV_CHUNK)
        sub = lax.broadcasted_iota(jnp.int32, (1, SUBLANES, CONV_CHUNK), 1)
        acc = jnp.zeros((n_tiles, SUBLANES, CONV_CHUNK), F32) + cb_ref[:, cols]
        for j in range(D_CONV):
            d = j - D_CONV // 2
            if d == 0:
                xs = xe[t0:t0 + n_tiles]
            else:
                rot = pltpu.roll(xe, (SUBLANES - d) % SUBLANES, axis=1)
                if d > 0:
                    xs = jnp.where(sub < SUBLANES - d, rot[t0:t0 + n_tiles], rot[t0 + 1:t0 + n_tiles + 1])
                else:
                    xs = jnp.where(sub < -d, rot[t0 - 1:t0 + n_tiles - 1], rot[t0:t0 + n_tiles])
            acc = acc + xs * cw_ref[j:j + 1, cols]
        xc_ref[:, cols] = _silu(acc).reshape(tm, CONV_CHUNK)

    n_chunks = CONV_DIM // CONV_CHUNK
    project_chunk(0)
    for c in range(n_chunks):
        if c + 1 < n_chunks:
            project_chunk(c + 1)
        conv_chunk(c)
    z_ref[...] = proj(_OFF_Z, D_INNER)
    g_ref[...] = jax.nn.sigmoid(proj(_OFF_G, 2 * D_MODEL) + bg_ref[...])
    dt_ref[...] = _softplus(proj(_OFF_DT, DT_PAD) + dtb_row_ref[...])
    dtt_ref[...] = _softplus(proj_t(_ROW_DT, DT_PAD) + dtb_col_ref[...])
    vt_ref[...] = proj_t(_ROW_V, KV_WIDTH).astype(BF16)

    half = HEAD_DIM // 2
    cos = cos_ref[...]
    sin = sin_ref[...]
    for h in range(N_KV_HEADS):
        x = _rms(proj(_OFF_K + h * HEAD_DIM, HEAD_DIM), kg_ref[...])
        k_ref[:, h * HEAD_DIM:(h + 1) * HEAD_DIM] = (x * cos + pltpu.roll(x, half, axis=1) * sin).astype(BF16)

    cos_t = cos_t_ref[...]
    sin_t = sin_t_ref[...]
    q_scale = HEAD_DIM ** -0.5 * LOG2_E
    for h in range(N_Q_HEADS):
        x = proj_t(_ROW_Q + h * HEAD_DIM, HEAD_DIM)
        x = x * lax.rsqrt(jnp.mean(x * x, axis=0, keepdims=True) + EPS) * qg_ref[...]
        rot = jnp.concatenate([x[half:], x[:half]], axis=0)
        qt_ref[h * HEAD_DIM:(h + 1) * HEAD_DIM, :] = ((x * cos_t + rot * sin_t) * q_scale).astype(BF16)


def _inproj(u, P, tables, *, tm=256):
    b, L, _ = u.shape

    def tok(width):
        return pl.BlockSpec((None, tm, width), lambda bi, i: (bi, i, 0))

    def tok_t(rows):
        return pl.BlockSpec((None, rows, tm), lambda bi, i: (bi, 0, i))

    pos = pl.BlockSpec((tm, HEAD_DIM), lambda bi, i: (i, 0))
    pos_t = pl.BlockSpec((HEAD_DIM, tm), lambda bi, i: (0, i))
    outs = [(tok, D_INNER, F32), (tok, CONV_DIM, F32), (tok_t, ATTN_WIDTH, BF16), (tok, KV_WIDTH, BF16),
            (tok_t, KV_WIDTH, BF16), (tok, 2 * D_MODEL, F32), (tok, DT_PAD, F32), (tok_t, DT_PAD, F32)]
    out_shape = [jax.ShapeDtypeStruct((b, L, w) if mk is tok else (b, w, L), dt) for mk, w, dt in outs]
    out_specs = [mk(w) for mk, w, _ in outs]
    cos, sin, cos_t, sin_t = tables
    r = tm // U_HALO
    last = L // U_HALO - 1
    halo_prev = pl.BlockSpec((None, U_HALO, D_MODEL), lambda bi, i: (bi, jnp.maximum(i * r - 1, 0), 0))
    halo_next = pl.BlockSpec((None, U_HALO, D_MODEL), lambda bi, i: (bi, jnp.minimum((i + 1) * r, last), 0))
    return pl.pallas_call(
        _inproj_kernel,
        grid=(b, L // tm),
        in_specs=[tok(D_MODEL), halo_prev, halo_next, _resident((D_MODEL, _IN_COLS)),
                  _resident((_T_ROWS, D_MODEL)), pos, pos,
                  pos_t, pos_t, _resident((HEAD_DIM, 1)), _resident((1, HEAD_DIM)), _resident((1, 2 * D_MODEL)),
                  _resident((1, DT_PAD)), _resident((DT_PAD, 1)),
                  _resident((SUBLANES, CONV_DIM)), _resident((1, CONV_DIM))],
        out_specs=out_specs,
        out_shape=out_shape,
        scratch_shapes=[pltpu.VMEM((tm + 2 * U_HALO, CONV_DIM), F32)],
        compiler_params=_params("parallel", "parallel"),
        name="in_proj",
    )(u, u, u, P["w_in"], P["w_in_t"], cos, sin, cos_t, sin_t, P["q_gain_col"], P["k_gain"], P["b_gate"],
      P["dt_bias_row"], P["dt_bias_col"], P["conv_w"], P["conv_b"])


DUP = 2 * SSM_HEADS


def _chunk_masks(rev):
    row_i = lax.broadcasted_iota(jnp.int32, (CHUNK, CHUNK), 0)
    col_i = lax.broadcasted_iota(jnp.int32, (CHUNK, CHUNK), 1)
    tri = (col_i >= row_i) if rev else (col_i <= row_i)
    return tri, tri.astype(F32).astype(BF16)


def _split3(x):
    hi = x.astype(BF16)
    r = x - hi.astype(F32)
    mid = r.astype(BF16)
    lo = (r - mid.astype(F32)).astype(BF16)
    return hi, mid, lo


def _ssd_prepare(dt2, dt_t, a2_row, a_col, bm, cm, expand_ref, tri_b, rev):
    tri3 = jnp.concatenate([tri_b, tri_b, tri_b], axis=1)
    acum2 = jnp.dot(tri3, jnp.concatenate(_split3(dt2 * a2_row), axis=0),
                    preferred_element_type=F32)
    acum_t = lax.dot_general(jnp.concatenate(_split3(dt_t * a_col), axis=1), tri3, (((1,), (1,)), ((), ())),
                             preferred_element_type=F32)
    end = 0 if rev else CHUNK - 1
    total2 = acum2[end:end + 1, :]
    ew = jnp.concatenate([jnp.exp2(acum2), jnp.exp2(total2 - acum2) * dt2], axis=0)
    ew_hi = ew.astype(BF16)
    ew_lo = (ew - ew_hi.astype(F32)).astype(BF16)
    lane = lax.broadcasted_iota(jnp.int32, ew.shape, 1)
    rep = jnp.dot(jnp.where(lane < SSM_HEADS, ew_hi, ew_lo), expand_ref[...],
                  preferred_element_type=F32)
    r_t = acum_t - jnp.log2(dt_t)
    b_gs = [bm[:, g * D_STATE:(g + 1) * D_STATE].astype(BF16) for g in range(SSM_GROUPS)]
    c_gs = [cm[:, g * D_STATE:(g + 1) * D_STATE].astype(BF16) for g in range(SSM_GROUPS)]
    cbs = []
    zero = jnp.zeros((CHUNK, D_STATE), BF16)
    for g in range(0, SSM_GROUPS, 2):
        b_diag = jnp.concatenate([jnp.concatenate([b_gs[g], zero], axis=1),
                                  jnp.concatenate([zero, b_gs[g + 1]], axis=1)], axis=0)
        cb2 = lax.dot_general(jnp.concatenate([c_gs[g], c_gs[g + 1]], axis=1), b_diag, (((1,), (1,)), ((), ())),
                              preferred_element_type=F32)
        cbs += [cb2[:, :CHUNK], cb2[:, CHUNK:]]
    return acum2, r_t, rep[:CHUNK], rep[CHUNK:], b_gs, c_gs, cbs


def _ssd_finish(prep, xs, state_ref, tri, rev):
    acum2, r_t, e_rep, w_rep, b_gs, c_gs, cbs = prep
    end = 0 if rev else CHUNK - 1
    cd_rep = e_rep[end:end + 1, :]
    y_offs = []
    for g in range(SSM_GROUPS):
        cols = slice(g * GROUP_WIDTH, (g + 1) * GROUP_WIDTH)
        prev = state_ref[g]
        y_offs.append(jnp.dot(c_gs[g], prev.astype(BF16), preferred_element_type=F32) * e_rep[:, cols])
        s_new = lax.dot_general(b_gs[g], (xs[:, cols] * w_rep[:, cols]).astype(BF16), (((0,), (0,)), ((), ())),
                                preferred_element_type=F32)
        state_ref[g] = prev * cd_rep[:, cols] + s_new
    lane = lax.broadcasted_iota(jnp.int32, (CHUNK, LANES), 1)
    lo_half = lane < SSM_HEAD_DIM
    ys = []
    for g in range(SSM_GROUPS):
        pairs = []
        for j in range(GROUP_WIDTH // LANES):
            ms = []
            for hh in range(2):
                h = g * (SSM_HEADS // SSM_GROUPS) + 2 * j + hh
                seg = acum2[:, h:h + 1] - r_t[h:h + 1, :]
                ms.append(cbs[g] * jnp.exp2(jnp.where(tri, seg, -jnp.inf)))
            lhs = jnp.concatenate(ms, axis=1).astype(BF16)
            xp = xs[:, g * GROUP_WIDTH + j * LANES:g * GROUP_WIDTH + (j + 1) * LANES]
            rhs = jnp.concatenate([jnp.where(lo_half, xp, 0.0), jnp.where(lo_half, 0.0, xp)],
                                  axis=0).astype(BF16)
            pairs.append(jnp.dot(lhs, rhs, preferred_element_type=F32))
        ys.append(jnp.concatenate(pairs, axis=1) + y_offs[g])
    return jnp.concatenate(ys, axis=1)


def _ssd_scan(xc_ref, dt_ref, dtt_ref, alog_row_ref, alog_col_ref, expand_ref, state_ref, emit, *, nchunk, rev):
    @pl.when(pl.program_id(1) == 0)
    def _():
        state_ref[...] = jnp.zeros_like(state_ref)

    lo, lo_t = (DUP, SSM_HEADS) if rev else (0, 0)
    a2_row = -jnp.exp(alog_row_ref[:, lo:lo + DUP]) * LOG2_E
    a_col = -jnp.exp(alog_col_ref[lo_t:lo_t + SSM_HEADS, :]) * LOG2_E
    tri, tri_b = _chunk_masks(rev)

    def prepare(c):
        rows = slice(c * CHUNK, (c + 1) * CHUNK)
        return _ssd_prepare(dt_ref[rows, lo:lo + DUP], dtt_ref[lo_t:lo_t + SSM_HEADS, rows], a2_row, a_col,
                            xc_ref[rows, D_INNER:D_INNER + BC_WIDTH], xc_ref[rows, D_INNER + BC_WIDTH:],
                            expand_ref, tri_b, rev)

    order = list(range(nchunk))[::-1] if rev else list(range(nchunk))
    nxt = prepare(order[0])
    for i, c in enumerate(order):
        cur = nxt
        if i + 1 < nchunk:
            nxt = prepare(order[i + 1])
        rows = slice(c * CHUNK, (c + 1) * CHUNK)
        xs = xc_ref[rows, :D_INNER]
        emit(rows, xs, _ssd_finish(cur, xs, state_ref, tri, rev))


def _ssd_fwd_kernel(xc_ref, dt_ref, dtt_ref, alog_row_ref, alog_col_ref, expand_ref, y_ref, state_ref, *, nchunk):
    def emit(rows, xs, y):
        y_ref[rows, :] = y

    _ssd_scan(xc_ref, dt_ref, dtt_ref, alog_row_ref, alog_col_ref, expand_ref, state_ref, emit,
              nchunk=nchunk, rev=False)


def _ssd_bwd_kernel(xc_ref, dt_ref, dtt_ref, alog_row_ref, alog_col_ref, expand_ref, yf_ref, z_ref, dskip_ref,
                    gain_ref, s_ref, state_ref, *, nchunk):
    def emit(rows, xs, y_b):
        y = (yf_ref[rows, :] + y_b + dskip_ref[...] * xs) * _silu(z_ref[rows, :])
        for g in range(SSM_GROUPS):
            cols = slice(g * GROUP_WIDTH, (g + 1) * GROUP_WIDTH)
            s_ref[rows, cols] = _rms(y[:, cols], gain_ref[:, cols]).astype(s_ref.dtype)

    _ssd_scan(xc_ref, dt_ref, dtt_ref, alog_row_ref, alog_col_ref, expand_ref, state_ref, emit,
              nchunk=nchunk, rev=True)


def _ssd(xc, dt, dt_t, z, P, *, nchunk=4):
    b, L, _ = xc.shape
    tl = nchunk * CHUNK
    nb = L // tl

    def tok(width, rev):
        if rev:
            return pl.BlockSpec((None, tl, width), lambda bi, c: (bi, nb - 1 - c, 0))
        return pl.BlockSpec((None, tl, width), lambda bi, c: (bi, c, 0))

    def tok_t(rev):
        if rev:
            return pl.BlockSpec((None, DT_PAD, tl), lambda bi, c: (bi, 0, nb - 1 - c))
        return pl.BlockSpec((None, DT_PAD, tl), lambda bi, c: (bi, 0, c))

    state = pltpu.VMEM((SSM_GROUPS, D_STATE, GROUP_WIDTH), F32)
    consts = [_resident((1, DT_PAD)), _resident((DT_PAD, 1)), _resident((DUP, D_INNER))]
    y_f = pl.pallas_call(
        functools.partial(_ssd_fwd_kernel, nchunk=nchunk),
        grid=(b, nb),
        in_specs=[tok(CONV_DIM, False), tok(DT_PAD, False), tok_t(False)] + consts,
        out_specs=tok(D_INNER, False),
        out_shape=jax.ShapeDtypeStruct((b, L, D_INNER), F32),
        scratch_shapes=[state],
        compiler_params=_params("parallel", "arbitrary"),
        name="ssd_fwd",
    )(xc, dt, dt_t, P["a_log_row"], P["a_log_col"], P["head_expand"])
    return pl.pallas_call(
        functools.partial(_ssd_bwd_kernel, nchunk=nchunk),
        grid=(b, nb),
        in_specs=[tok(CONV_DIM, True), tok(DT_PAD, True), tok_t(True)] + consts
                 + [tok(D_INNER, True), tok(D_INNER, True), _resident((1, D_INNER)), _resident((1, D_INNER))],
        out_specs=tok(D_INNER, True),
        out_shape=jax.ShapeDtypeStruct((b, L, D_INNER), BF16),
        scratch_shapes=[state],
        compiler_params=_params("parallel", "arbitrary"),
        name="ssd_bwd",
    )(xc, dt, dt_t, P["a_log_row"], P["a_log_col"], P["head_expand"], y_f, z, P["d_skip_rep"], P["ssm_norm"])


ONES_ROWS = 16
QK_LOOKAHEAD = 3


def _attn_kernel(qt_ref, k_ref, vt_ref, o_ref, m_ref, acc_ref, *, tq, nb):
    kv = pl.program_id(3)

    @pl.when(kv == 0)
    def _():
        m_ref[...] = jnp.full_like(m_ref, -jnp.inf)
        acc_ref[...] = jnp.zeros_like(acc_ref)

    k = k_ref[...]
    vt = vt_ref[...]
    vt_aug = jnp.concatenate([vt, jnp.ones((ONES_ROWS, vt.shape[1]), BF16)], axis=0)
    blocks = [(g, c) for g in range(Q_PER_KV) for c in range(tq // nb)]

    def scores(g, c):
        qt = qt_ref[g * HEAD_DIM:(g + 1) * HEAD_DIM, c * nb:(c + 1) * nb]
        return jnp.dot(k, qt, preferred_element_type=F32)

    pending = [scores(*blk) for blk in blocks[:QK_LOOKAHEAD]]
    for i, (g, c) in enumerate(blocks):
        s = pending.pop(0)
        if i + QK_LOOKAHEAD < len(blocks):
            pending.append(scores(*blocks[i + QK_LOOKAHEAD]))
        cols = slice(g * tq + c * nb, g * tq + (c + 1) * nb)
        m_prev = m_ref[:, cols]
        m_new = jnp.maximum(m_prev, jnp.max(s, axis=0, keepdims=True))
        alpha = jnp.exp2(m_prev - m_new)
        p = jnp.exp2(s - m_new).astype(BF16)
        acc_ref[:, cols] = alpha * acc_ref[:, cols] + jnp.dot(vt_aug, p, preferred_element_type=F32)
        m_ref[:, cols] = m_new

    @pl.when(kv == pl.num_programs(3) - 1)
    def _():
        for g in range(Q_PER_KV):
            acc = acc_ref[:, g * tq:(g + 1) * tq]
            out_t = acc[:HEAD_DIM] / acc[HEAD_DIM:HEAD_DIM + 1]
            o_ref[:, g * HEAD_DIM:(g + 1) * HEAD_DIM] = out_t.T.astype(o_ref.dtype)


def _attn_bounded_kernel(qt_ref, k_ref, vt_ref, o_ref, acc_ref, l_ref, *, tq, nb, kb):
    kv = pl.program_id(3)

    @pl.when(kv == 0)
    def _():
        acc_ref[...] = jnp.zeros_like(acc_ref)
        l_ref[...] = jnp.zeros_like(l_ref)

    tk = k_ref.shape[0]
    blocks = [(t, g, c) for t in range(tk // kb) for g in range(Q_PER_KV) for c in range(tq // nb)]

    def scores(t, g, c):
        qt = qt_ref[g * HEAD_DIM:(g + 1) * HEAD_DIM, c * nb:(c + 1) * nb]
        return jnp.dot(k_ref[t * kb:(t + 1) * kb, :], qt, preferred_element_type=F32)

    pending = [scores(*blk) for blk in blocks[:QK_LOOKAHEAD]]
    for i, (t, g, c) in enumerate(blocks):
        s = pending.pop(0)
        if i + QK_LOOKAHEAD < len(blocks):
            pending.append(scores(*blocks[i + QK_LOOKAHEAD]))
        cols = slice(g * tq + c * nb, g * tq + (c + 1) * nb)
        p = jnp.exp2(s)
        l_ref[:, cols] += jnp.sum(p.reshape(kb // SUBLANES, SUBLANES, nb), axis=0)
        acc_ref[:, cols] += jnp.dot(vt_ref[:, t * kb:(t + 1) * kb], p.astype(BF16), preferred_element_type=F32)

    @pl.when(kv == pl.num_programs(3) - 1)
    def _():
        for g in range(Q_PER_KV):
            cols = slice(g * tq, (g + 1) * tq)
            out_t = acc_ref[:, cols] / jnp.sum(l_ref[:, cols], axis=0, keepdims=True)
            o_ref[:, g * HEAD_DIM:(g + 1) * HEAD_DIM] = out_t.T.astype(o_ref.dtype)


def _attention(qt, k, vt, *, bounded, tq, tk, nb=256):
    b, L, _ = k.shape
    gw = Q_PER_KV * HEAD_DIM
    if bounded:
        body = functools.partial(_attn_bounded_kernel, kb=min(tk, 512))
        scratch = [pltpu.VMEM((HEAD_DIM, Q_PER_KV * tq), F32), pltpu.VMEM((SUBLANES, Q_PER_KV * tq), F32)]
    else:
        body = _attn_kernel
        scratch = [pltpu.VMEM((1, Q_PER_KV * tq), F32), pltpu.VMEM((HEAD_DIM + ONES_ROWS, Q_PER_KV * tq), F32)]
    return pl.pallas_call(
        functools.partial(body, tq=tq, nb=nb),
        grid=(b, N_KV_HEADS, L // tq, L // tk),
        in_specs=[pl.BlockSpec((None, gw, tq), lambda bi, h, i, j: (bi, h, i)),
                  pl.BlockSpec((None, tk, HEAD_DIM), lambda bi, h, i, j: (bi, j, h)),
                  pl.BlockSpec((None, HEAD_DIM, tk), lambda bi, h, i, j: (bi, h, j))],
        out_specs=pl.BlockSpec((None, tq, gw), lambda bi, h, i, j: (bi, i, h)),
        out_shape=jax.ShapeDtypeStruct((b, L, ATTN_WIDTH), BF16),
        scratch_shapes=scratch,
        compiler_params=_params("parallel", "parallel", "parallel", "arbitrary"),
        name="attn_bounded" if bounded else "attn",
    )(qt, k, vt)


SCORE_BOUND = 40.0
_BF16_NORM_SLACK = (1.0 + 2.0 ** -8) ** 2


def _attention_dispatch(qt, k, vt, q_gain, k_gain):
    L = k.shape[1]
    bound = (LOG2_E * HEAD_DIM ** 0.5 * _BF16_NORM_SLACK) * jnp.max(jnp.abs(q_gain)) * jnp.max(jnp.abs(k_gain))
    tq_fast = next(t for t in (2048, 1024, 512) if L % t == 0)
    return lax.cond(bound <= SCORE_BOUND,
                    functools.partial(_attention, bounded=True, tq=tq_fast, tk=tq_fast),
                    functools.partial(_attention, bounded=False, tq=512, tk=512),
                    qt, k, vt)


def _rope_tables(L):
    rows = L // GRID_W
    row = jnp.repeat(jnp.arange(rows, dtype=F32), GRID_W)
    col = jnp.tile(jnp.arange(GRID_W, dtype=F32), rows)
    axis_dim = HEAD_DIM // 2
    inv_freq = ROPE_THETA ** (-jnp.arange(0, axis_dim, 2, dtype=F32) / axis_dim)
    ang = jnp.concatenate([row[:, None] * inv_freq, col[:, None] * inv_freq], axis=-1)
    cos, sin = jnp.cos(ang), jnp.sin(ang)
    cos2, sin2 = jnp.concatenate([cos, cos], axis=-1), jnp.concatenate([-sin, sin], axis=-1)
    return cos2, sin2, cos2.T, sin2.T


def _layer_params(norm_ffn1, w_ffn1_gu, w_ffn1_down, norm_mix, w_in, conv_w, conv_b, dt_bias_f, dt_bias_b,
                  A_log_f, A_log_b, D_skip, ssm_norm, q_norm, k_norm, w_ssm_branch, w_attn_branch, b_gate,
                  w_out, norm_ffn2, w_ffn2_gu, w_ffn2_down):
    sizes = (D_INNER, CONV_DIM, SSM_HEADS, SSM_HEADS, ATTN_WIDTH, KV_WIDTH, KV_WIDTH, 2 * D_MODEL)
    splits = [int(v) for v in np.cumsum(sizes)[:-1]]
    w_z, w_xbc, w_dtf, w_dtb, w_q, w_k, w_v, w_g = jnp.split(w_in, splits, axis=-1)
    perm = np.concatenate([np.arange(0, HEAD_DIM, 2), np.arange(1, HEAD_DIM, 2)])
    q_cols = np.concatenate([h * HEAD_DIM + perm for h in range(N_Q_HEADS)])
    k_cols = np.concatenate([h * HEAD_DIM + perm for h in range(N_KV_HEADS)])
    w_dt_dup = jnp.concatenate([w_dtf, w_dtf, w_dtb, w_dtb], axis=-1)
    w_dt_pad = jnp.concatenate([w_dtf, w_dtb, jnp.zeros((D_MODEL, DT_PAD - 2 * SSM_HEADS), F32)], axis=-1)
    w_r = jnp.concatenate([w_z, w_xbc, w_k[:, k_cols], w_g, w_dt_dup], axis=-1)
    w_t = jnp.concatenate([w_q[:, q_cols], w_v, w_dt_pad], axis=-1).T
    pad = jnp.zeros((DT_PAD - 2 * SSM_HEADS,), F32)
    head_expand = jnp.repeat(jnp.tile(jnp.eye(SSM_HEADS, dtype=BF16), (2, 1)), SSM_HEAD_DIM, axis=1)
    row = lambda v: v.reshape(1, -1).astype(F32)
    return dict(
        norm_ffn1=row(norm_ffn1), w_ffn1_gu=w_ffn1_gu.astype(BF16), w_ffn1_down=w_ffn1_down.astype(BF16),
        norm_mix=row(norm_mix), w_in=w_r.astype(BF16), w_in_t=w_t.astype(BF16),
        conv_w=jnp.concatenate([conv_w, jnp.zeros((SUBLANES - D_CONV, CONV_DIM), F32)], axis=0),
        conv_b=row(conv_b),
        dt_bias_row=row(jnp.concatenate([dt_bias_f, dt_bias_f, dt_bias_b, dt_bias_b])),
        dt_bias_col=jnp.concatenate([dt_bias_f, dt_bias_b, pad]).reshape(-1, 1),
        a_log_row=row(jnp.concatenate([A_log_f, A_log_f, A_log_b, A_log_b])),
        a_log_col=jnp.concatenate([A_log_f, A_log_b, pad]).reshape(-1, 1), head_expand=head_expand,
        d_skip_rep=row(jnp.repeat(D_skip, SSM_HEAD_DIM)), ssm_norm=row(ssm_norm),
        q_gain_col=q_norm[perm].reshape(-1, 1).astype(F32), k_gain=row(k_norm[perm]),
        w_ssm_branch=w_ssm_branch.astype(BF16), w_attn_branch=w_attn_branch.astype(BF16),
        b_gate=row(b_gate), w_out=w_out.astype(BF16),
        norm_ffn2=row(norm_ffn2), w_ffn2_gu=w_ffn2_gu.astype(BF16), w_ffn2_down=w_ffn2_down.astype(BF16),
    )


def _encoder_layer(x, P, g_next, last, tables):
    L = x.shape[1]
    cos, sin, cos_t, sin_t = tables
    tables = (cos[:L], sin[:L], cos_t[:, :L], sin_t[:, :L])
    h, u = _ffn(x, P["norm_ffn1"], P["w_ffn1_gu"], P["w_ffn1_down"], P["norm_mix"],
                emit_h=True, post_dtype=BF16)
    z, xc, q_t, k, v_t, g, dt, dt_t = _inproj(u, P, tables)
    s = _ssd(xc, dt, dt_t, z, P)
    a = _attention_dispatch(q_t, k, v_t, P["q_gain_col"], P["k_gain"])
    outs = _merge_ffn(s, a, g, h, P, g_next, emit_h=not last, post_dtype=F32)
    return outs


def _trunk(x, layers, norm_final, tables):
    depth = len(layers)
    for i, P in enumerate(layers):
        last = i == depth - 1
        g_next = norm_final.reshape(1, -1) if last else layers[i + 1]["norm_ffn1"]
        outs = _encoder_layer(x, P, g_next, last, tables)
        x = outs[0]
    return x


def kernel(x_prompt, x_sample, norm_ffn1, w_ffn1_gu, w_ffn1_down, norm_mix, w_in, conv_w, conv_b, dt_bias_f,
           dt_bias_b, A_log_f, A_log_b, D_skip, ssm_norm, q_norm, k_norm, w_ssm_branch, w_attn_branch, b_gate,
           w_out, norm_ffn2, w_ffn2_gu, w_ffn2_down, norm_final):
    per_layer = (norm_ffn1, w_ffn1_gu, w_ffn1_down, norm_mix, w_in, conv_w, conv_b, dt_bias_f, dt_bias_b,
                 A_log_f, A_log_b, D_skip, ssm_norm, q_norm, k_norm, w_ssm_branch, w_attn_branch, b_gate,
                 w_out, norm_ffn2, w_ffn2_gu, w_ffn2_down)
    depth = norm_ffn1.shape[0]
    layers = [_layer_params(*(w[i] for w in per_layer)) for i in range(depth)]
    tables = _rope_tables(max(x_prompt.shape[1], x_sample.shape[1]))
    return (_trunk(x_prompt, layers, norm_final, tables), _trunk(x_sample, layers, norm_final, tables))
```
